```python
import jax, jax.numpy as jnp
from jax import lax
import numpy as np


D_MODEL = 2048
BATCH = 1
SEQ = 8192
DEPTH = 2

EPS = 1e-6
NEG = -1e30
QB = 128

MLA_HEADS = 8
MLA_Q_RANK = 384
MLA_KV_RANK = 128
MLA_NOPE = 64
MLA_ROPE = 32
MLA_V = 64
MLA_QK = MLA_NOPE + MLA_ROPE
ROPE_THETA = 10000.0

DIL_HEADS = 12
DIL_HD = 64
DIL_W = DIL_HEADS * DIL_HD
DIL_PATTERNS = ((128, 1), (512, 4), (2048, 16))
N_BUCKETS = 32
BUCKET_MAX_DIST = 1024

ML_HEADS = 4
ML_HD = 192
ML_W = ML_HEADS * ML_HD
ML_CONV = 5
ML_CHUNK = 128
ML_F_BIAS_LO = 3.0
ML_F_BIAS_HI = 6.0

N_GROUPS = 4
EXPERTS_PER_GROUP = 4
N_EXPERTS = N_GROUPS * EXPERTS_PER_GROUP
MOE_TOPK = 2
D_EXPERT = 512

IN_SIZES = (MLA_Q_RANK, MLA_KV_RANK, MLA_ROPE, DIL_W, DIL_W, DIL_W, ML_W, ML_W, ML_W, 4 * ML_HEADS)
C_IN = sum(IN_SIZES)
D_MIX = MLA_HEADS * MLA_V + DIL_W + ML_W

kernel_name = 'hybrid_mla_dilated_mlstm_hmoe_encoder'


def rms_norm(x, g):
    xf = x.astype(jnp.float32)
    y = xf * lax.rsqrt(jnp.mean(xf * xf, axis=-1, keepdims=True) + EPS)
    return (y * g.astype(jnp.float32)).astype(x.dtype)


def rope_cos_sin(positions):
    half = MLA_ROPE // 2
    inv_freq = ROPE_THETA ** (-jnp.arange(half, dtype=jnp.float32) / half)
    ang = positions.astype(jnp.float32)[..., None] * inv_freq
    return jnp.cos(ang), jnp.sin(ang)


def apply_rope(x, cos, sin):
    xf = x.astype(jnp.float32)
    x1, x2 = jnp.split(xf, 2, axis=-1)
    return jnp.concatenate([x1 * cos - x2 * sin, x1 * sin + x2 * cos], axis=-1).astype(x.dtype)


def blocks_to_seq(out):
    nb, B, H, L, d = out.shape
    return out.transpose(1, 0, 3, 2, 4).reshape(B, nb * L, H * d)


def dense_block_attention(q, k, v):
    S = q.shape[2]
    scale = q.shape[-1] ** -0.5

    def one_block(start):
        qb = lax.dynamic_slice_in_dim(q, start, QB, axis=2)
        s = jnp.einsum('bhqd,bhkd->bhqk', qb, k, preferred_element_type=jnp.float32) * scale
        p = jax.nn.softmax(s, axis=-1)
        return jnp.einsum('bhqk,bhkd->bhqd', p.astype(v.dtype), v)

    return blocks_to_seq(lax.map(one_block, jnp.arange(S // QB, dtype=jnp.int32) * QB))


def mla_mixer(c_q, c_kv, k_rope, cos, sin, g_cq, w_uq, g_ckv, w_ukv, g_q, g_k):
    B, S, _ = c_q.shape
    q = (rms_norm(c_q, g_cq) @ w_uq).reshape(B, S, MLA_HEADS, MLA_QK)
    kv = (rms_norm(c_kv, g_ckv) @ w_ukv).reshape(B, S, MLA_HEADS, MLA_NOPE + MLA_V)
    q_rot = apply_rope(q[..., MLA_NOPE:], cos[:, :, None], sin[:, :, None])
    k_rot = apply_rope(k_rope, cos, sin)
    q = jnp.concatenate([q[..., :MLA_NOPE], q_rot], axis=-1)
    k = jnp.concatenate([kv[..., :MLA_NOPE],
                         jnp.broadcast_to(k_rot[:, :, None, :], (B, S, MLA_HEADS, MLA_ROPE))], axis=-1)
    v = kv[..., MLA_NOPE:]
    q = rms_norm(q, g_q).transpose(0, 2, 1, 3)
    k = rms_norm(k, g_k).transpose(0, 2, 1, 3)
    return dense_block_attention(q, k, v.transpose(0, 2, 1, 3))


def dilation_offsets():
    offs = []
    for w, d in DIL_PATTERNS:
        r = w // (2 * d)
        offs.append(np.arange(-r, r + 1) * d)
    return np.stack(offs)


def t5_buckets(rel):
    half = N_BUCKETS // 2
    max_exact = half // 2
    n = np.abs(rel)
    large = max_exact + (np.log(np.maximum(n, 1) / max_exact) / np.log(BUCKET_MAX_DIST / max_exact)
                         * (half - max_exact)).astype(np.int32)
    large = np.minimum(large, half - 1)
    return (rel > 0).astype(np.int32) * half + np.where(n < max_exact, n, large).astype(np.int32)


def dilated_mixer(q, k, v, rel_bias, g_q, g_k):
    B, S, _ = q.shape
    q = rms_norm(q.reshape(B, S, DIL_HEADS, DIL_HD), g_q).transpose(0, 2, 1, 3)
    k = rms_norm(k.reshape(B, S, DIL_HEADS, DIL_HD), g_k).transpose(0, 2, 1, 3)
    v = v.reshape(B, S, DIL_HEADS, DIL_HD).transpose(0, 2, 1, 3)
    offs = dilation_offsets()
    P, K = offs.shape
    flat = jnp.asarray(offs.reshape(-1), jnp.int32)
    bias = rel_bias.astype(jnp.float32)[jnp.asarray(t5_buckets(offs.reshape(-1)))].T
    scale = DIL_HD ** -0.5

    def one_block(start):
        qb = lax.dynamic_slice_in_dim(q, start, QB, axis=2)
        idx = start + jnp.arange(QB, dtype=jnp.int32)[:, None] + flat[None, :]
        valid = (idx >= 0) & (idx < S)
        idx = jnp.clip(idx, 0, S - 1)
        kg = k[:, :, idx]
        vg = v[:, :, idx].reshape(B, DIL_HEADS, QB, P, K, DIL_HD).astype(jnp.float32)
        s = jnp.einsum('bhqd,bhqkd->bhqk', qb, kg, preferred_element_type=jnp.float32) * scale
        s = jnp.where(valid[None, None], s + bias[None, :, None, :], NEG).reshape(B, DIL_HEADS, QB, P, K)
        m = jnp.max(s, axis=-1)
        p = jnp.exp(s - m[..., None])
        den = jnp.sum(p, axis=-1)
        o = jnp.einsum('bhqpk,bhqpkd->bhqpd', p, vg) / den[..., None]
        alpha = jax.nn.softmax(jnp.log(den) + m, axis=-1)
        return jnp.einsum('bhqp,bhqpd->bhqd', alpha, o).astype(v.dtype)

    return blocks_to_seq(lax.map(one_block, jnp.arange(S // QB, dtype=jnp.int32) * QB))


def mlstm_scan(q, k, v, log_i, log_f):
    B, H, S, d = q.shape
    L = ML_CHUNK
    nc = S // L

    def to_chunks(a):
        return jnp.moveaxis(a.reshape(B, H, nc, L, *a.shape[3:]), 2, 0)

    xs = (to_chunks(q), to_chunks(k), to_chunks(v), to_chunks(log_i), to_chunks(log_f))
    tril = jnp.tril(jnp.ones((L, L), dtype=bool))

    def step(carry, inp):
        C, n, m_prev = carry
        qc, kc, vc, li, lf = inp
        b = jnp.cumsum(lf, axis=-1)
        logw = jnp.where(tril, b[..., :, None] - b[..., None, :] + li[..., None, :], NEG)
        inter = b + m_prev[..., None]
        m = jnp.maximum(jnp.max(logw, axis=-1), inter)
        a = jnp.exp(logw - m[..., None]) * jnp.einsum('bhtd,bhsd->bhts', qc, kc)
        decay = jnp.exp(inter - m)
        num = jnp.einsum('bhts,bhsv->bhtv', a, vc) + decay[..., None] * jnp.einsum('bhtd,bhdv->bhtv', qc, C)
        den = jnp.sum(a, axis=-1) + decay * jnp.einsum('bhtd,bhd->bht', qc, n)
        h = num / jnp.maximum(jnp.abs(den), jnp.exp(-m))[..., None]
        m_new = m[..., -1]
        g = jnp.exp(b[..., -1:] - b + li - m_new[..., None])
        cd = jnp.exp(b[..., -1] + m_prev - m_new)
        C = cd[..., None, None] * C + jnp.einsum('bhs,bhsd,bhsv->bhdv', g, kc, vc)
        n = cd[..., None] * n + jnp.einsum('bhs,bhsd->bhd', g, kc)
        return (C, n, m_new), h

    init = (jnp.zeros((B, H, d, d), jnp.float32), jnp.zeros((B, H, d), jnp.float32),
            jnp.zeros((B, H), jnp.float32))
    _, hs = lax.scan(step, init, xs)
    return jnp.moveaxis(hs, 0, 2).reshape(B, H, S, d)


def mlstm_mixer(u, v, o_pre, gate_pre, conv_w, conv_b, w_q, w_k, gate_bias, g_out):
    B, S, _ = u.shape
    f32 = jnp.float32
    uc = lax.conv_general_dilated(u, conv_w[:, None, :].astype(u.dtype), window_strides=(1,),
                                  padding=((ML_CONV // 2, ML_CONV // 2),),
                                  dimension_numbers=('NWC', 'WIO', 'NWC'), feature_group_count=ML_W)
    uc = jax.nn.silu(uc + conv_b).reshape(B, S, ML_HEADS, ML_HD)
    q = jnp.einsum('bshd,hde->bhse', uc, w_q).astype(f32)
    k = jnp.einsum('bshd,hde->bhse', uc, w_k).astype(f32) * (ML_HD ** -0.5)
    vv = v.reshape(B, S, ML_HEADS, ML_HD).transpose(0, 2, 1, 3).astype(f32)
    pre = gate_pre.astype(f32).reshape(B, S, 2, 2, ML_HEADS) + gate_bias.astype(f32)
    log_i = jnp.transpose(pre[:, :, :, 0], (2, 0, 3, 1))
    log_f = jnp.transpose(jax.nn.log_sigmoid(pre[:, :, :, 1]), (2, 0, 3, 1))
    h_fwd = mlstm_scan(q, k, vv, log_i[0], log_f[0])
    rev = lambda a: jnp.flip(a, axis=2)
    h_bwd = rev(mlstm_scan(rev(q), rev(k), rev(vv), rev(log_i[1]), rev(log_f[1])))
    h = (h_fwd + h_bwd).transpose(0, 2, 1, 3)
    h = rms_norm(h, g_out.reshape(ML_HEADS, ML_HD)) * jax.nn.sigmoid(o_pre.astype(f32)).reshape(B, S, ML_HEADS, ML_HD)
    return h.reshape(B, S, ML_W).astype(u.dtype)


def hier_moe(x, wg, bg, we, be, w_gate, w_up, w_down):
    B, S, _ = x.shape
    f32 = jnp.float32
    g_logits = (x @ wg).astype(f32) + bg.astype(f32)
    grp = jnp.argmax(g_logits, axis=-1)
    p_grp = jnp.max(jax.nn.softmax(g_logits, axis=-1), axis=-1)
    e_logits = ((x @ we).astype(f32) + be.astype(f32)).reshape(B, S, N_GROUPS, EXPERTS_PER_GROUP)
    e_sel = jnp.take_along_axis(e_logits, grp[..., None, None], axis=2)[..., 0, :]
    top_v, top_i = lax.top_k(e_sel, MOE_TOPK)
    w = jax.nn.softmax(top_v, axis=-1) * p_grp[..., None]
    eid = grp[..., None] * EXPERTS_PER_GROUP + top_i
    comb = jnp.sum(jax.nn.one_hot(eid, N_EXPERTS, dtype=f32) * w[..., None], axis=-2)
    h = jax.nn.silu(jnp.einsum('bsd,edf->bsef', x, w_gate)) * jnp.einsum('bsd,edf->bsef', x, w_up)
    h = h * comb[..., None].astype(h.dtype)
    return jnp.einsum('bsef,efd->bsd', h, w_down)


def setup_inputs(seed: int = 0) -> dict:
    key = jax.random.key(seed)
    ks = jax.random.split(key, 32)
    f32 = jnp.float32
    L = DEPTH

    def nrm(k, shape, scale):
        return jax.random.normal(k, shape, f32) * scale

    def gain(k, shape):
        return 1.0 + 0.05 * jax.random.normal(k, shape, f32)

    i_bias = nrm(ks[16], (L, 2, 1, ML_HEADS), 0.1)
    f_bias = jnp.linspace(ML_F_BIAS_LO, ML_F_BIAS_HI, ML_HEADS, dtype=f32) + nrm(ks[17], (L, 2, 1, ML_HEADS), 0.1)
    return {
        'x': nrm(ks[0], (BATCH, SEQ, D_MODEL), 1.0),
        'positions': jnp.arange(SEQ, dtype=jnp.int32)[None, :]
                     + jax.random.randint(ks[1], (BATCH, 1), 0, 4096, dtype=jnp.int32),
        'rel_bias': nrm(ks[2], (N_BUCKETS, DIL_HEADS), 0.2),
        'norm_mix': gain(ks[3], (L, D_MODEL)),
        'w_in': nrm(ks[4], (L, D_MODEL, C_IN), D_MODEL ** -0.5),
        'mla_norm_cq': gain(ks[5], (L, MLA_Q_RANK)),
        'mla_w_uq': nrm(ks[6], (L, MLA_Q_RANK, MLA_HEADS * MLA_QK), MLA_Q_RANK ** -0.5),
        'mla_norm_ckv': gain(ks[7], (L, MLA_KV_RANK)),
        'mla_w_ukv': nrm(ks[8], (L, MLA_KV_RANK, MLA_HEADS * (MLA_NOPE + MLA_V)), MLA_KV_RANK ** -0.5),
        'mla_q_norm': gain(ks[9], (L, MLA_QK)),
        'mla_k_norm': gain(ks[10], (L, MLA_QK)),
        'dil_q_norm': gain(ks[11], (L, DIL_HD)),
        'dil_k_norm': gain(ks[12], (L, DIL_HD)),
        'ml_conv_w': nrm(ks[13], (L, ML_CONV, ML_W), ML_CONV ** -0.5),
        'ml_conv_b': nrm(ks[14], (L, ML_W), 0.02),
        'ml_w_q': nrm(ks[15], (L, ML_HEADS, ML_HD, ML_HD), ML_HD ** -0.5),
        'ml_w_k': nrm(ks[18], (L, ML_HEADS, ML_HD, ML_HD), ML_HD ** -0.5),
        'ml_gate_bias': jnp.concatenate([i_bias, f_bias], axis=2),
        'ml_out_norm': gain(ks[19], (L, ML_W)),
        'w_out': nrm(ks[20], (L, D_MIX, D_MODEL), D_MIX ** -0.5),
        'norm_ffn': gain(ks[21], (L, D_MODEL)),
        'router_group_w': nrm(ks[22], (L, D_MODEL, N_GROUPS), D_MODEL ** -0.5),
        'router_group_b': nrm(ks[23], (L, N_GROUPS), 0.01),
        'router_expert_w': nrm(ks[24], (L, D_MODEL, N_EXPERTS), D_MODEL ** -0.5),
        'router_expert_b': nrm(ks[25], (L, N_EXPERTS), 0.01),
        'moe_w_gate': nrm(ks[26], (L, N_EXPERTS, D_MODEL, D_EXPERT), D_MODEL ** -0.5),
        'moe_w_up': nrm(ks[27], (L, N_EXPERTS, D_MODEL, D_EXPERT), D_MODEL ** -0.5),
        'moe_w_down': nrm(ks[28], (L, N_EXPERTS, D_EXPERT, D_MODEL), D_EXPERT ** -0.5),
    }


def reference(x, positions, rel_bias, norm_mix, w_in, mla_norm_cq, mla_w_uq, mla_norm_ckv, mla_w_ukv,
              mla_q_norm, mla_k_norm, dil_q_norm, dil_k_norm, ml_conv_w, ml_conv_b, ml_w_q, ml_w_k,
              ml_gate_bias, ml_out_norm, w_out, norm_ffn, router_group_w, router_group_b,
              router_expert_w, router_expert_b, moe_w_gate, moe_w_up, moe_w_down):
    cos, sin = rope_cos_sin(positions)
    split_at = np.cumsum(IN_SIZES)[:-1].tolist()
    for l in range(DEPTH):
        xn = rms_norm(x, norm_mix[l])
        c_q, c_kv, k_rope, dq, dk, dv, mu, mv, mo, mg = jnp.split(xn @ w_in[l], split_at, axis=-1)
        y_mla = mla_mixer(c_q, c_kv, k_rope, cos, sin, mla_norm_cq[l], mla_w_uq[l], mla_norm_ckv[l],
                          mla_w_ukv[l], mla_q_norm[l], mla_k_norm[l])
        y_dil = dilated_mixer(dq, dk, dv, rel_bias, dil_q_norm[l], dil_k_norm[l])
        y_ml = mlstm_mixer(mu, mv, mo, mg, ml_conv_w[l], ml_conv_b[l], ml_w_q[l], ml_w_k[l],
                           ml_gate_bias[l], ml_out_norm[l])
        x = x + jnp.concatenate([y_mla, y_dil, y_ml], axis=-1) @ w_out[l]
        x = x + hier_moe(rms_norm(x, norm_ffn[l]), router_group_w[l], router_group_b[l],
                         router_expert_w[l], router_expert_b[l], moe_w_gate[l], moe_w_up[l], moe_w_down[l])
    return x
```

```python
import functools
import math

import numpy as np
import jax
import jax.numpy as jnp
from jax import lax
from jax.experimental import pallas as pl
from jax.experimental.pallas import tpu as pltpu

F32 = jnp.float32
BF16 = jnp.bfloat16

D_MODEL = 2048
EPS = 1e-6
NEG = -1e30
LOG2E = 1.4426950408889634

MLA_HEADS = 8
MLA_Q_RANK = 384
MLA_KV_RANK = 128
MLA_NOPE = 64
MLA_ROPE = 32
MLA_V = 64
MLA_QK = MLA_NOPE + MLA_ROPE
ROPE_THETA = 10000.0

DIL_HEADS = 12
DIL_HD = 64
DIL_W = DIL_HEADS * DIL_HD
DIL_PATTERNS = ((128, 1), (512, 4), (2048, 16))
DIL_R = 64
N_BUCKETS = 32
BUCKET_MAX_DIST = 1024

ML_HEADS = 4
ML_HD = 192
ML_HP = 256
ML_W = ML_HEADS * ML_HD
ML_WP = ML_HEADS * ML_HP
ML_CONV = 5
ML_CHUNK = 128

N_GROUPS = 4
EXPERTS_PER_GROUP = 4
N_EXPERTS = 16
D_EXPERT = 512

LANES = 128
HALO = 8
MAIN_W = 6144
VMEM_LIMIT = 56 * 1024 * 1024


def _cparams(sem):
    return pltpu.CompilerParams(dimension_semantics=sem, vmem_limit_bytes=VMEM_LIMIT)


def _lane(shape):
    return lax.broadcasted_iota(jnp.int32, shape, len(shape) - 1)


def _split3(x):
    hi = x.astype(BF16)
    r1 = x - hi.astype(F32)
    mid = r1.astype(BF16)
    lo = (r1 - mid.astype(F32)).astype(BF16)
    return hi, mid, lo


def _dot(a, b):
    return jnp.dot(a, b, preferred_element_type=F32)


def _dot_nt(a, b):
    return lax.dot_general(a, b, (((1,), (1,)), ((), ())), preferred_element_type=F32)


def _dot_tn(a, b):
    return lax.dot_general(a, b, (((0,), (0,)), ((), ())), preferred_element_type=F32)


def _inproj_kernel(x_ref, g_ref, w_ref, wg_ref, o_ref, og_ref, xn_ref):
    @pl.when(pl.program_id(1) == 0)
    def _():
        x = x_ref[...]
        ms = jnp.mean(x * x, axis=-1, keepdims=True)
        xn = (x * lax.rsqrt(ms + EPS) * g_ref[...]).astype(BF16)
        xn_ref[...] = xn
        og_ref[...] = _dot(xn, wg_ref[...])

    o_ref[...] = _dot(xn_ref[...], w_ref[...]).astype(o_ref.dtype)


def _inproj(x, g, w_main, w_gate, tm=1024, tn=768):
    S, D = x.shape
    N = w_main.shape[1]
    return pl.pallas_call(
        _inproj_kernel,
        grid=(S // tm, N // tn),
        in_specs=[
            pl.BlockSpec((tm, D), lambda i, j: (i, 0)),
            pl.BlockSpec((1, D), lambda i, j: (0, 0)),
            pl.BlockSpec((D, tn), lambda i, j: (0, j)),
            pl.BlockSpec((D, LANES), lambda i, j: (0, 0)),
        ],
        out_specs=[
            pl.BlockSpec((tm, tn), lambda i, j: (i, j)),
            pl.BlockSpec((tm, LANES), lambda i, j: (i, 0)),
        ],
        out_shape=[jax.ShapeDtypeStruct((S, N), BF16), jax.ShapeDtypeStruct((S, LANES), F32)],
        scratch_shapes=[pltpu.VMEM((tm, D), BF16)],
        compiler_params=_cparams(("parallel", "arbitrary")),
        name="inproj",
    )(x, g, w_main, w_gate)


def _rope_tables(pos_ref, invf_ref):
    ang = pos_ref[...].astype(F32) * invf_ref[...]
    lane = _lane(ang.shape)
    cos = jnp.cos(ang)
    sin = jnp.sin(ang)
    rope_a = (lane >= MLA_NOPE) & (lane < MLA_NOPE + MLA_ROPE // 2)
    rope_b = (lane >= MLA_NOPE + MLA_ROPE // 2) & (lane < MLA_QK)
    c = jnp.where(lane < MLA_NOPE, 1.0, jnp.where(lane < MLA_QK, cos, 0.0))
    s1 = jnp.where(rope_a, -sin, 0.0)
    s2 = jnp.where(rope_b, sin, 0.0)
    return c, s1, s2


def _rope(x, tabs):
    c, s1, s2 = tabs
    half = MLA_ROPE // 2
    return x * c + pltpu.roll(x, LANES - half, 1) * s1 + pltpu.roll(x, half, 1) * s2


def _mla_prep_kernel(in_ref, pos_ref, invf_ref, gcq_ref, wq_ref, gckv_ref, wk_ref, wv_ref, vone_ref,
                     gq_ref, gk_ref, q_ref, k_ref, v_ref):
    xin = in_ref[...].astype(F32)
    cq = xin[:, :MLA_Q_RANK]
    ckv = xin[:, MLA_Q_RANK:MLA_Q_RANK + MLA_KV_RANK]
    kr_blk = xin[:, MLA_Q_RANK + MLA_KV_RANK:MLA_Q_RANK + MLA_KV_RANK + LANES]

    cqn = (cq * lax.rsqrt(jnp.mean(cq * cq, axis=-1, keepdims=True) + EPS) * gcq_ref[...]).astype(BF16)
    ckvn = (ckv * lax.rsqrt(jnp.mean(ckv * ckv, axis=-1, keepdims=True) + EPS) * gckv_ref[...]).astype(BF16)
    q = _dot(cqn, wq_ref[...])
    kn = _dot(ckvn, wk_ref[...])
    v = _dot(ckvn, wv_ref[...]) + vone_ref[...]
    v_ref[...] = v.astype(v_ref.dtype)

    tabs = _rope_tables(pos_ref, invf_ref)
    lane = _lane(kr_blk.shape)
    kr = jnp.where((lane >= MLA_NOPE) & (lane < MLA_QK), pltpu.roll(kr_blk, MLA_NOPE, 1), 0.0)
    kr = _rope(kr, tabs)

    q_scale = (MLA_QK ** -0.5) * LOG2E
    for h in range(MLA_HEADS):
        sl = slice(h * LANES, (h + 1) * LANES)
        qh = _rope(q[:, sl], tabs)
        ss = jnp.sum(qh * qh, axis=-1, keepdims=True) * (1.0 / MLA_QK)
        q_ref[:, sl] = (qh * lax.rsqrt(ss + EPS) * gq_ref[...] * q_scale).astype(q_ref.dtype)
        kh = kn[:, sl] + kr
        ss = jnp.sum(kh * kh, axis=-1, keepdims=True) * (1.0 / MLA_QK)
        k_ref[:, sl] = (kh * lax.rsqrt(ss + EPS) * gk_ref[...]).astype(k_ref.dtype)


def _mla_prep(main, pos, invf, gcq, wq, gckv, wk, wv, vone, gq, gk, tm=512):
    S = main.shape[0]
    HW = MLA_HEADS * LANES
    full = lambda shape: pl.BlockSpec(shape, lambda i: (0, 0))
    out = jax.ShapeDtypeStruct((S, HW), BF16)
    return pl.pallas_call(
        _mla_prep_kernel,
        grid=(S // tm,),
        in_specs=[
            pl.BlockSpec((tm, 768), lambda i: (i, 3)),
            pl.BlockSpec((tm, 1), lambda i: (i, 0)),
            full((1, LANES)), full((1, MLA_Q_RANK)), full((MLA_Q_RANK, HW)), full((1, MLA_KV_RANK)),
            full((MLA_KV_RANK, HW)), full((MLA_KV_RANK, HW)), full((1, HW)), full((1, LANES)), full((1, LANES)),
        ],
        out_specs=[pl.BlockSpec((tm, HW), lambda i: (i, 0))] * 3,
        out_shape=[out, out, out],
        compiler_params=_cparams(("parallel",)),
        name="mla_prep",
    )(main, pos, invf, gcq, wq, gckv, wk, wv, vone, gq, gk)


def _mla_attn_kernel(q_ref, k_ref, v_ref, o_ref, *, tk):
    tq = q_ref.shape[0]
    S = k_ref.shape[0]
    outs = []
    for hh in range(2):
        sl = slice(hh * LANES, (hh + 1) * LANES)
        q = q_ref[:, sl]

        def body(c, carry, sl=sl, q=q):
            m, acc = carry
            rows = pl.ds(pl.multiple_of(c * tk, tk), tk)
            s = _dot_nt(q, k_ref[rows, sl])
            m_new = jnp.maximum(m, jnp.max(s, axis=-1, keepdims=True))
            p = jnp.exp2(s - m_new)
            acc = acc * jnp.exp2(m - m_new) + _dot(p.astype(BF16), v_ref[rows, sl])
            return m_new, acc

        m0 = jnp.full((tq, 1), NEG, F32)
        acc0 = jnp.zeros((tq, LANES), F32)
        _, acc = lax.fori_loop(0, S // tk, body, (m0, acc0))
        outs.append(acc[:, :MLA_V] / acc[:, MLA_V:MLA_V + 1])
    o_ref[...] = jnp.concatenate(outs, axis=-1).astype(o_ref.dtype)


def _mla_attn(q, k, v, tq=256, tk=512):
    S = q.shape[0]
    return pl.pallas_call(
        functools.partial(_mla_attn_kernel, tk=tk),
        grid=(MLA_HEADS // 2, S // tq),
        in_specs=[
            pl.BlockSpec((tq, 2 * LANES), lambda h, i: (i, h)),
            pl.BlockSpec((S, 2 * LANES), lambda h, i: (0, h)),
            pl.BlockSpec((S, 2 * LANES), lambda h, i: (0, h)),
        ],
        out_specs=pl.BlockSpec((tq, 2 * MLA_V), lambda h, i: (i, h)),
        out_shape=jax.ShapeDtypeStruct((S, MLA_HEADS * MLA_V), BF16),
        compiler_params=_cparams(("parallel", "parallel")),
        name="mla_attn",
    )(q, k, v)


def _dil_norm_kernel(q_ref, k_ref, gq_ref, gk_ref, qo_ref, ko_ref):
    def norm(x_ref, g_ref, o_ref, scale):
        for b in range(DIL_W // LANES):
            sl = slice(b * LANES, (b + 1) * LANES)
            x = x_ref[:, sl].astype(F32)
            lo = _lane(x.shape) < DIL_HD
            sq = x * x
            ss_lo = jnp.sum(jnp.where(lo, sq, 0.0), axis=-1, keepdims=True)
            ss_hi = jnp.sum(jnp.where(lo, 0.0, sq), axis=-1, keepdims=True)
            r = lax.rsqrt(jnp.where(lo, ss_lo, ss_hi) * (1.0 / DIL_HD) + EPS)
            o_ref[:, sl] = (x * r * g_ref[...] * scale).astype(o_ref.dtype)

    norm(q_ref, gq_ref, qo_ref, (DIL_HD ** -0.5) * LOG2E)
    norm(k_ref, gk_ref, ko_ref, 1.0)


def _dil_norm(main, gq, gk, tm=1024):
    S = main.shape[0]
    out = jax.ShapeDtypeStruct((S, DIL_W), BF16)
    return pl.pallas_call(
        _dil_norm_kernel,
        grid=(S // tm,),
        in_specs=[
            pl.BlockSpec((tm, DIL_W), lambda i: (i, 0)),
            pl.BlockSpec((tm, DIL_W), lambda i: (i, 1)),
            pl.BlockSpec((1, LANES), lambda i: (0, 0)),
            pl.BlockSpec((1, LANES), lambda i: (0, 0)),
        ],
        out_specs=[pl.BlockSpec((tm, DIL_W), lambda i: (i, 0))] * 2,
        out_shape=[out, out],
        compiler_params=_cparams(("parallel",)),
        name="dil_norm",
    )(main, main, gq, gk)


def _dil_band_kernel(q_ref, kp_ref, kc_ref, kn_ref, vp_ref, vc_ref, vn_ref, b_ref, o_ref, st_ref):
    ub = pl.program_id(1)
    nub = pl.num_programs(1)
    tq = q_ref.shape[0]
    win = tq + 2 * DIL_R
    col = _lane((1, win))
    prev_pen = jnp.where(ub > 0, 0.0, NEG)
    next_pen = jnp.where(ub < nub - 1, 0.0, NEG)
    edge = jnp.where(col < DIL_R, prev_pen, jnp.where(col >= DIL_R + tq, next_pen, 0.0))
    kwin = jnp.concatenate([kp_ref[...], kc_ref[...], kn_ref[...]], axis=0)
    vwin = jnp.concatenate([vp_ref[...], vc_ref[...], vn_ref[...]], axis=0)
    q = q_ref[...]
    st_lane = _lane((tq, LANES))
    st = jnp.zeros((tq, LANES), F32)
    for h in range(DIL_HEADS):
        sl = slice(h * DIL_HD, (h + 1) * DIL_HD)
        s = _dot_nt(q[:, sl], kwin[:, sl]) + b_ref[h] + edge
        m = jnp.max(s, axis=-1, keepdims=True)
        p = jnp.exp2(s - m)
        den = jnp.sum(p, axis=-1, keepdims=True)
        o = _dot(p.astype(BF16), vwin[:, sl]) / den
        o_ref[:, sl] = o.astype(o_ref.dtype)
        st = jnp.where(st_lane == h, m, jnp.where(st_lane == h + LANES // 2, den, st))
    st_ref[...] = st


def _dil_band(qn, kn, main, bias, d, tq=128):
    S = qn.shape[0]
    sub = S // d
    nub = sub // tq
    hb = tq // DIL_R
    mb = MAIN_W // DIL_W
    qv = qn.reshape(sub, d * DIL_W)
    kv = kn.reshape(sub, d * DIL_W)
    mv = main.reshape(sub, d * MAIN_W)
    last = sub // DIL_R - 1
    prev = lambda u: jnp.maximum(u * hb - 1, 0)
    nxt = lambda u: jnp.minimum(u * hb + hb, last)
    o, st = pl.pallas_call(
        _dil_band_kernel,
        grid=(d, nub),
        in_specs=[
            pl.BlockSpec((tq, DIL_W), lambda r, u: (u, r)),
            pl.BlockSpec((DIL_R, DIL_W), lambda r, u: (prev(u), r)),
            pl.BlockSpec((tq, DIL_W), lambda r, u: (u, r)),
            pl.BlockSpec((DIL_R, DIL_W), lambda r, u: (nxt(u), r)),
            pl.BlockSpec((DIL_R, DIL_W), lambda r, u: (prev(u), r * mb + 2)),
            pl.BlockSpec((tq, DIL_W), lambda r, u: (u, r * mb + 2)),
            pl.BlockSpec((DIL_R, DIL_W), lambda r, u: (nxt(u), r * mb + 2)),
            pl.BlockSpec((DIL_HEADS, tq, tq + 2 * DIL_R), lambda r, u: (0, 0, 0)),
        ],
        out_specs=[
            pl.BlockSpec((tq, DIL_W), lambda r, u: (u, r)),
            pl.BlockSpec((tq, LANES), lambda r, u: (u, r)),
        ],
        out_shape=[jax.ShapeDtypeStruct((sub, d * DIL_W), BF16), jax.ShapeDtypeStruct((sub, d * LANES), F32)],
        compiler_params=_cparams(("parallel", "parallel")),
        name=f"dil_band{d}",
    )(qv, kv, kv, kv, mv, mv, mv, bias)
    return o.reshape(S, DIL_W), st.reshape(S, LANES)


def _log_sigmoid(x):
    return jnp.minimum(x, 0.0) - jnp.log(1.0 + jnp.exp(-jnp.abs(x)))


def _ml_prep_kernel(up_ref, uc_ref, un_ref, cw_ref, cb_ref, wqk_ref, g_ref, gb_ref,
                    q_ref, k_ref, gcol_ref, grow_ref):
    i = pl.program_id(0)
    tm = uc_ref.shape[0]
    pm = jnp.where(i > 0, 1.0, 0.0)
    nm = jnp.where(i < pl.num_programs(0) - 1, 1.0, 0.0)
    xc = jnp.concatenate([up_ref[...].astype(F32) * pm, uc_ref[...].astype(F32), un_ref[...].astype(F32) * nm],
                         axis=0)
    acc = jnp.zeros((tm, ML_WP), F32) + cb_ref[...]
    for j in range(ML_CONV):
        off = HALO - ML_CONV // 2 + j
        acc = acc + xc[off:off + tm] * cw_ref[j:j + 1, :]
    u = (acc * jax.nn.sigmoid(acc)).astype(BF16)
    qk = _dot(u, wqk_ref[...])
    q_ref[...] = qk[:, :ML_WP].astype(q_ref.dtype)
    k_ref[...] = qk[:, ML_WP:].astype(k_ref.dtype)

    g = g_ref[...] + gb_ref[...]
    lane = _lane(g.shape)
    is_f = (lane % 8) >= ML_HEADS
    gcol = jnp.where(lane < 4 * ML_HEADS, jnp.where(is_f, _log_sigmoid(g), g), 0.0)
    gcol_ref[...] = gcol
    grow_ref[...] = gcol.T


def _ml_prep(main, gates, cw, cb, wqk, gb, tm=512):
    S = main.shape[0]
    hb = tm // HALO
    last = S // HALO - 1
    full = lambda shape: pl.BlockSpec(shape, lambda i: (0, 0))
    return pl.pallas_call(
        _ml_prep_kernel,
        grid=(S // tm,),
        in_specs=[
            pl.BlockSpec((HALO, ML_WP), lambda i: (jnp.maximum(i * hb - 1, 0), 3)),
            pl.BlockSpec((tm, ML_WP), lambda i: (i, 3)),
            pl.BlockSpec((HALO, ML_WP), lambda i: (jnp.minimum(i * hb + hb, last), 3)),
            full((HALO, ML_WP)), full((1, ML_WP)), full((ML_WP, 2 * ML_WP)),
            pl.BlockSpec((tm, LANES), lambda i: (i, 0)),
            full((1, LANES)),
        ],
        out_specs=[
            pl.BlockSpec((tm, ML_WP), lambda i: (i, 0)),
            pl.BlockSpec((tm, ML_WP), lambda i: (i, 0)),
            pl.BlockSpec((tm, LANES), lambda i: (i, 0)),
            pl.BlockSpec((LANES, tm), lambda i: (0, i)),
        ],
        out_shape=[
            jax.ShapeDtypeStruct((S, ML_WP), BF16), jax.ShapeDtypeStruct((S, ML_WP), BF16),
            jax.ShapeDtypeStruct((S, LANES), F32), jax.ShapeDtypeStruct((LANES, S), F32),
        ],
        compiler_params=_cparams(("parallel",)),
        name="ml_prep",
    )(main, main, main, cw, cb, wqk, gates, gb)


def _ml_scan_kernel(qf_ref, kf_ref, vf_ref, gcf_ref, grf_ref, qb_ref, kb_ref, vb_ref, gcb_ref, grb_ref,
                    hf_ref, hb_ref, c_ref, n_ref, m_ref):
    L = ML_CHUNK

    @pl.when(pl.program_id(0) == 0)
    def _():
        c_ref[...] = jnp.zeros_like(c_ref)
        n_ref[...] = jnp.zeros_like(n_ref)
        m_ref[...] = jnp.zeros_like(m_ref)

    row = lax.broadcasted_iota(jnp.int32, (L, L), 0)
    colm = lax.broadcasted_iota(jnp.int32, (L, L), 1)
    lower = colm <= row
    upper = colm >= row
    t_lower = jnp.where(lower, 1.0, 0.0).astype(BF16)
    t_upper = jnp.where(upper, 1.0, 0.0).astype(BF16)

    def tri_left(t, x):
        hi, mid, lo = _split3(x)
        return _dot(t, hi) + _dot(t, mid) + _dot(t, lo)

    def tri_right(x, t):
        hi, mid, lo = _split3(x)
        return _dot(hi, t) + _dot(mid, t) + _dot(lo, t)

    dirs = (
        (qf_ref, kf_ref, vf_ref, gcf_ref, grf_ref, hf_ref, t_lower, t_upper, lower, L - 1),
        (qb_ref, kb_ref, vb_ref, gcb_ref, grb_ref, hb_ref, t_upper, t_lower, upper, 0),
    )
    for dd, (q_ref, k_ref, v_ref, gc_ref, gr_ref, h_ref, t_col, t_row, mask, last) in enumerate(dirs):
        gcol = gc_ref[...]
        grow = gr_ref[...]
        bcol = tri_left(t_col, gcol)
        brow = tri_right(grow, t_row)
        for h in range(ML_HEADS):
            ch = dd * ML_HEADS + h
            li = dd * 2 * ML_HEADS + h
            lf = li + ML_HEADS
            sl = slice(h * ML_HP, (h + 1) * ML_HP)
            q = q_ref[:, sl]
            k = k_ref[:, sl]
            v = v_ref[:, sl]
            b_c = bcol[:, lf:lf + 1]
            b_r = brow[lf:lf + 1, :]
            li_c = gcol[:, li:li + 1]
            li_r = grow[li:li + 1, :]
            m_prev = m_ref[ch:ch + 1, 0:1]
            c_prev = c_ref[ch]
            n_prev = n_ref[ch]

            logw = jnp.where(mask, b_c - b_r + li_r, NEG)
            inter = b_c + m_prev
            m = jnp.maximum(jnp.max(logw, axis=-1, keepdims=True), inter)
            a = jnp.exp(logw - m) * _dot_nt(q, k)
            decay = jnp.exp(inter - m)
            num = _dot(a.astype(BF16), v) + decay * _dot(q, c_prev.astype(BF16))
            qn = jnp.sum(q.astype(F32) * n_prev, axis=-1, keepdims=True)
            den = jnp.sum(a, axis=-1, keepdims=True) + decay * qn
            h_ref[:, sl] = num / jnp.maximum(jnp.abs(den), jnp.exp(-m))

            m_new = m[last:last + 1, :]
            b_last = b_c[last:last + 1, :]
            g = jnp.exp(b_last - b_c + li_c - m_new)
            cd = jnp.exp(b_last + m_prev - m_new)
            kg = k.astype(F32) * g
            c_ref[ch] = cd * c_prev + _dot_tn(kg.astype(BF16), v)
            n_ref[ch] = cd * n_prev + jnp.sum(kg, axis=0, keepdims=True)
            m_ref[ch:ch + 1, :] = jnp.broadcast_to(m_new, (1, LANES))


def _ml_scan(q, k, main, gcol, grow):
    S = q.shape[0]
    L = ML_CHUNK
    nc = S // L
    fwd = lambda j: (j, 0)
    bwd = lambda j: (nc - 1 - j, 0)
    blk = lambda f: pl.BlockSpec((L, ML_WP), f)
    vblk = lambda f: pl.BlockSpec((L, ML_WP), lambda j: (f(j)[0], 4))
    out = jax.ShapeDtypeStruct((S, ML_WP), F32)
    return pl.pallas_call(
        _ml_scan_kernel,
        grid=(nc,),
        in_specs=[
            blk(fwd), blk(fwd), vblk(fwd), pl.BlockSpec((L, LANES), fwd), pl.BlockSpec((LANES, L), lambda j: (0, j)),
            blk(bwd), blk(bwd), vblk(bwd), pl.BlockSpec((L, LANES), bwd),
            pl.BlockSpec((LANES, L), lambda j: (0, nc - 1 - j)),
        ],
        out_specs=[blk(fwd), blk(bwd)],
        out_shape=[out, out],
        scratch_shapes=[
            pltpu.VMEM((2 * ML_HEADS, ML_HP, ML_HP), F32),
            pltpu.VMEM((2 * ML_HEADS, 1, ML_HP), F32),
            pltpu.VMEM((2 * ML_HEADS, LANES), F32),
        ],
        compiler_params=_cparams(("arbitrary",)),
        name="ml_scan",
    )(q, k, main, gcol, grow, q, k, main, gcol, grow)


def _outproj_kernel(x_ref, ymla_ref, o1_ref, o2_ref, o3_ref, s1_ref, s2_ref, s3_ref, hf_ref, hb_ref, mo_ref,
                    gout_ref, e_ref, w1_ref, w2_ref, w3_ref, out_ref):
    sts = [s1_ref[...], s2_ref[...], s3_ref[...]]
    m_all = jnp.maximum(jnp.maximum(sts[0], sts[1]), sts[2])
    ws = [jnp.exp2(st - m_all) * pltpu.roll(st, LANES // 2, 1) for st in sts]
    inv = 1.0 / (ws[0] + ws[1] + ws[2])
    lane = _lane(inv.shape)
    ydil = jnp.zeros(o1_ref.shape, F32)
    for w, o_ref in zip(ws, (o1_ref, o2_ref, o3_ref)):
        alpha = jnp.where(lane < DIL_HEADS, w * inv, 0.0)
        hi, mid, lo = _split3(alpha)
        e = e_ref[...]
        ydil = ydil + (_dot(hi, e) + _dot(mid, e) + _dot(lo, e)) * o_ref[...].astype(F32)

    hs = hf_ref[...] + hb_ref[...]
    yml = []
    for h in range(ML_HEADS):
        sl = slice(h * ML_HP, (h + 1) * ML_HP)
        hh = hs[:, sl]
        ss = jnp.sum(hh * hh, axis=-1, keepdims=True) * (1.0 / ML_HD)
        yml.append(hh * lax.rsqrt(ss + EPS) * gout_ref[:, sl] * jax.nn.sigmoid(mo_ref[:, sl].astype(F32)))
    yml = jnp.concatenate(yml, axis=-1)

    out_ref[...] = (x_ref[...] + _dot(ymla_ref[...], w1_ref[...]) + _dot(ydil.astype(BF16), w2_ref[...])
                    + _dot(yml.astype(BF16), w3_ref[...]))


def _outproj(x, ymla, dil_o, dil_st, hf, hb, main, gout, e, w1, w2, w3, tm=512):
    S, D = x.shape
    row = lambda w: pl.BlockSpec((tm, w), lambda i: (i, 0))
    full = lambda shape: pl.BlockSpec(shape, lambda i: (0, 0))
    return pl.pallas_call(
        _outproj_kernel,
        grid=(S // tm,),
        in_specs=[
            row(D), row(MLA_HEADS * MLA_V), row(DIL_W), row(DIL_W), row(DIL_W), row(LANES), row(LANES), row(LANES),
            row(ML_WP), row(ML_WP), pl.BlockSpec((tm, ML_WP), lambda i: (i, 5)),
            full((1, ML_WP)), full((LANES, DIL_W)), full((MLA_HEADS * MLA_V, D)), full((DIL_W, D)), full((ML_WP, D)),
        ],
        out_specs=row(D),
        out_shape=jax.ShapeDtypeStruct((S, D), F32),
        compiler_params=_cparams(("parallel",)),
        name="outproj",
    )(x, ymla, *dil_o, *dil_st, hf, hb, main, gout, e, w1, w2, w3)


def _router_kernel(x_ref, g_ref, wr_ref, br_ref, xn_ref, comb_ref):
    x = x_ref[...]
    xn = x * lax.rsqrt(jnp.mean(x * x, axis=-1, keepdims=True) + EPS) * g_ref[...]
    xn_ref[...] = xn.astype(xn_ref.dtype)
    x1, x2, x3 = _split3(xn)
    w1 = wr_ref[0]
    w2 = wr_ref[1]
    w3 = wr_ref[2]
    logits = (_dot(x1, w1) + (_dot(x1, w2) + _dot(x2, w1)) + (_dot(x1, w3) + _dot(x2, w2) + _dot(x3, w1))
              + br_ref[...])
    lane = _lane(logits.shape)
    big = jnp.int32(4 * LANES)

    def first_max(mask):
        v = jnp.max(jnp.where(mask, logits, NEG), axis=-1, keepdims=True)
        idx = jnp.min(jnp.where(mask & (logits == v), lane, big), axis=-1, keepdims=True)
        return v, idx

    gmask = lane < N_GROUPS
    gmax, grp = first_max(gmask)
    p_grp = 1.0 / jnp.sum(jnp.where(gmask, jnp.exp(logits - gmax), 0.0), axis=-1, keepdims=True)
    e_lo = N_GROUPS + grp * EXPERTS_PER_GROUP
    emask = (lane >= e_lo) & (lane < e_lo + EXPERTS_PER_GROUP)
    v1, i1 = first_max(emask)
    v2, i2 = first_max(emask & (lane != i1))
    t = jnp.exp(v2 - v1)
    w_1 = p_grp / (1.0 + t)
    w_2 = p_grp * t / (1.0 + t)
    comb_ref[...] = jnp.where(lane == i1 - N_GROUPS, w_1, 0.0) + jnp.where(lane == i2 - N_GROUPS, w_2, 0.0)


def _router(x, g, wr, br, tm=512):
    S, D = x.shape
    return pl.pallas_call(
        _router_kernel,
        grid=(S // tm,),
        in_specs=[
            pl.BlockSpec((tm, D), lambda i: (i, 0)),
            pl.BlockSpec((1, D), lambda i: (0, 0)),
            pl.BlockSpec((3, D, LANES), lambda i: (0, 0, 0)),
            pl.BlockSpec((1, LANES), lambda i: (0, 0)),
        ],
        out_specs=[pl.BlockSpec((tm, D), lambda i: (i, 0)), pl.BlockSpec((tm, LANES), lambda i: (i, 0))],
        out_shape=[jax.ShapeDtypeStruct((S, D), BF16), jax.ShapeDtypeStruct((S, LANES), F32)],
        compiler_params=_cparams(("parallel",)),
        name="router",
    )(x, g, wr, br)


def _moe_kernel(x_ref, xn_ref, comb_ref, wg_ref, wu_ref, wd_ref, out_ref):
    e = pl.program_id(1)

    @pl.when(e == 0)
    def _():
        out_ref[...] = x_ref[...]

    xn = xn_ref[...]
    comb = comb_ref[...]
    c = jnp.sum(jnp.where(_lane(comb.shape) == e, comb, 0.0), axis=-1, keepdims=True)
    gate = _dot(xn, wg_ref[0])
    up = _dot(xn, wu_ref[0])
    hid = gate * jax.nn.sigmoid(gate) * up * c
    out_ref[...] += _dot(hid.astype(BF16), wd_ref[0])


def _moe(x, xn, comb, wg, wu, wd, tm=512):
    S, D = x.shape
    return pl.pallas_call(
        _moe_kernel,
        grid=(S // tm, N_EXPERTS),
        in_specs=[
            pl.BlockSpec((tm, D), lambda i, e: (i, 0)),
            pl.BlockSpec((tm, D), lambda i, e: (i, 0)),
            pl.BlockSpec((tm, LANES), lambda i, e: (i, 0)),
            pl.BlockSpec((1, D, D_EXPERT), lambda i, e: (e, 0, 0)),
            pl.BlockSpec((1, D, D_EXPERT), lambda i, e: (e, 0, 0)),
            pl.BlockSpec((1, D_EXPERT, D), lambda i, e: (e, 0, 0)),
        ],
        out_specs=pl.BlockSpec((tm, D), lambda i, e: (i, 0)),
        out_shape=jax.ShapeDtypeStruct((S, D), F32),
        compiler_params=_cparams(("parallel", "arbitrary")),
        name="moe",
    )(x, xn, comb, wg, wu, wd)


def _pad_cols(w, width):
    return jnp.pad(w, ((0, 0), (0, width - w.shape[1])))


def _pad_heads(w):
    lead = w.shape[:-1]
    w = w.reshape(*lead, ML_HEADS, ML_HD)
    w = jnp.pad(w, [(0, 0)] * len(lead) + [(0, 0), (0, ML_HP - ML_HD)])
    return w.reshape(*lead, ML_WP)


def _t5_buckets(rel):
    half = N_BUCKETS // 2
    max_exact = half // 2
    n = np.abs(rel)
    large = max_exact + (np.log(np.maximum(n, 1) / max_exact) / np.log(BUCKET_MAX_DIST / max_exact)
                         * (half - max_exact)).astype(np.int32)
    large = np.minimum(large, half - 1)
    return (rel > 0).astype(np.int32) * half + np.where(n < max_exact, n, large).astype(np.int32)


def _band_bias(rel_bias, d, tq=128):
    win = tq + 2 * DIL_R
    j = np.arange(win)[None, :] - DIL_R - np.arange(tq)[:, None]
    inside = np.abs(j) <= DIL_R
    buckets = _t5_buckets(np.where(inside, j, 0) * d)
    b = rel_bias.astype(F32)[jnp.asarray(buckets)] * LOG2E
    b = jnp.where(jnp.asarray(inside)[..., None], b, NEG)
    return jnp.transpose(b, (2, 0, 1))


def _layer_weights(l, p):
    w_in = p["w_in"][l]
    off = np.cumsum((0, MLA_Q_RANK, MLA_KV_RANK, MLA_ROPE, DIL_W, DIL_W, DIL_W, ML_W, ML_W, ML_W, 4 * ML_HEADS))
    col = lambda a, b: w_in[:, off[a]:off[b]]
    w_main = jnp.concatenate([
        col(3, 6),
        _pad_cols(col(0, 3), DIL_W),
        _pad_cols(col(6, 7), ML_WP),
        _pad_heads(col(7, 8)), _pad_heads(col(8, 9)),
    ], axis=1).astype(BF16)
    w_gate = _pad_cols(col(9, 10), LANES).astype(BF16)

    wq = p["mla_w_uq"][l].reshape(MLA_Q_RANK, MLA_HEADS, MLA_QK)
    wq = jnp.pad(wq, ((0, 0), (0, 0), (0, LANES - MLA_QK))).reshape(MLA_Q_RANK, MLA_HEADS * LANES).astype(BF16)
    wkv = p["mla_w_ukv"][l].reshape(MLA_KV_RANK, MLA_HEADS, MLA_NOPE + MLA_V)
    padh = lambda w: jnp.pad(w, ((0, 0), (0, 0), (0, LANES - w.shape[-1]))).reshape(
        MLA_KV_RANK, MLA_HEADS * LANES).astype(BF16)
    wk = padh(wkv[..., :MLA_NOPE])
    wv = padh(wkv[..., MLA_NOPE:])
    vone = jnp.tile((jnp.arange(LANES) == MLA_V).astype(F32), MLA_HEADS)[None, :]
    lane_pad = lambda g: jnp.pad(g, (0, LANES - g.shape[0]))[None, :]

    def block_diag(w):
        out = jnp.zeros((ML_WP, ML_WP), F32)
        for h in range(ML_HEADS):
            out = out.at[h * ML_HD:(h + 1) * ML_HD, h * ML_HP:h * ML_HP + ML_HD].set(w[h])
        return out
    wqk = jnp.concatenate([block_diag(p["ml_w_q"][l]), block_diag(p["ml_w_k"][l]) * (ML_HD ** -0.5)],
                          axis=1).astype(BF16)
    cw = jnp.pad(p["ml_conv_w"][l], ((0, HALO - ML_CONV), (0, ML_WP - ML_W)))
    cb = _pad_cols(p["ml_conv_b"][l][None, :], ML_WP)
    gb = _pad_cols(p["ml_gate_bias"][l].reshape(1, 4 * ML_HEADS), LANES)

    w_out = p["w_out"][l]
    n_mla = MLA_HEADS * MLA_V
    w3 = w_out[n_mla + DIL_W:].reshape(ML_HEADS, ML_HD, D_MODEL)
    w3 = jnp.pad(w3, ((0, 0), (0, ML_HP - ML_HD), (0, 0))).reshape(ML_WP, D_MODEL)

    wr = jnp.concatenate([p["router_group_w"][l], p["router_expert_w"][l]], axis=1)
    wr = jnp.stack(_split3(_pad_cols(wr, LANES)))
    br = _pad_cols(jnp.concatenate([p["router_group_b"][l], p["router_expert_b"][l]])[None, :], LANES)
    return dict(
        norm_mix=p["norm_mix"][l][None, :], w_main=w_main, w_gate=w_gate,
        gcq=p["mla_norm_cq"][l][None, :], wq=wq, gckv=p["mla_norm_ckv"][l][None, :], wk=wk, wv=wv, vone=vone,
        gq=lane_pad(p["mla_q_norm"][l]), gk=lane_pad(p["mla_k_norm"][l]),
        dgq=jnp.tile(p["dil_q_norm"][l], 2)[None, :], dgk=jnp.tile(p["dil_k_norm"][l], 2)[None, :],
        cw=cw, cb=cb, wqk=wqk, gb=gb, gout=_pad_heads(p["ml_out_norm"][l][None, :]),
        w1=w_out[:n_mla].astype(BF16), w2=w_out[n_mla:n_mla + DIL_W].astype(BF16), w3=w3.astype(BF16),
        norm_ffn=p["norm_ffn"][l][None, :], wr=wr, br=br,
        moe_wg=p["moe_w_gate"][l].astype(BF16), moe_wu=p["moe_w_up"][l].astype(BF16),
        moe_wd=p["moe_w_down"][l].astype(BF16),
    )


def kernel(x, positions, rel_bias, norm_mix, w_in, mla_norm_cq, mla_w_uq, mla_norm_ckv, mla_w_ukv, mla_q_norm, mla_k_norm, dil_q_norm, dil_k_norm, ml_conv_w, ml_conv_b, ml_w_q, ml_w_k, ml_gate_bias, ml_out_norm, w_out, norm_ffn, router_group_w, router_group_b, router_expert_w, router_expert_b, moe_w_gate, moe_w_up, moe_w_down):
    params = dict(norm_mix=norm_mix, w_in=w_in, mla_norm_cq=mla_norm_cq, mla_w_uq=mla_w_uq,
                  mla_norm_ckv=mla_norm_ckv, mla_w_ukv=mla_w_ukv, mla_q_norm=mla_q_norm, mla_k_norm=mla_k_norm,
                  dil_q_norm=dil_q_norm, dil_k_norm=dil_k_norm, ml_conv_w=ml_conv_w, ml_conv_b=ml_conv_b,
                  ml_w_q=ml_w_q, ml_w_k=ml_w_k, ml_gate_bias=ml_gate_bias, ml_out_norm=ml_out_norm, w_out=w_out,
                  norm_ffn=norm_ffn, router_group_w=router_group_w, router_group_b=router_group_b,
                  router_expert_w=router_expert_w, router_expert_b=router_expert_b, moe_w_gate=moe_w_gate,
                  moe_w_up=moe_w_up, moe_w_down=moe_w_down)
    B, S, D = x.shape
    assert B == 1 and D == D_MODEL
    depth = w_in.shape[0]
    xs = x.reshape(S, D)
    pos = positions.reshape(S, 1)

    half = MLA_ROPE // 2
    inv_freq = ROPE_THETA ** (-jnp.arange(half, dtype=F32) / half)
    invf = jnp.zeros((LANES,), F32).at[MLA_NOPE:MLA_NOPE + half].set(inv_freq)
    invf = invf.at[MLA_NOPE + half:MLA_QK].set(inv_freq)[None, :]
    biases = [_band_bias(rel_bias, d) for _, d in DIL_PATTERNS]
    expand = (jnp.arange(LANES)[:, None] == (jnp.arange(DIL_W) // DIL_HD)[None, :]).astype(BF16)

    for l in range(depth):
        w = _layer_weights(l, params)
        main, gates = _inproj(xs, w["norm_mix"], w["w_main"], w["w_gate"])
        q, k, v = _mla_prep(main, pos, invf, w["gcq"], w["wq"], w["gckv"], w["wk"], w["wv"], w["vone"],
                            w["gq"], w["gk"])
        ymla = _mla_attn(q, k, v)
        qn, kn = _dil_norm(main, w["dgq"], w["dgk"])
        dil = [_dil_band(qn, kn, main, b, d) for b, (_, d) in zip(biases, DIL_PATTERNS)]
        mq, mk, gcol, grow = _ml_prep(main, gates, w["cw"], w["cb"], w["wqk"], w["gb"])
        hf, hb = _ml_scan(mq, mk, main, gcol, grow)
        xs = _outproj(xs, ymla, [o for o, _ in dil], [s for _, s in dil], hf, hb, main, w["gout"], expand,
                      w["w1"], w["w2"], w["w3"])
        xn, comb = _router(xs, w["norm_ffn"], w["wr"], w["br"])
        xs = _moe(xs, xn, comb, w["moe_wg"], w["moe_wu"], w["moe_wd"])
    return xs.reshape(B, S, D)
```

```python
import functools
import math

import numpy as np
import jax
import jax.numpy as jnp
from jax import lax
from jax.experimental import pallas as pl
from jax.experimental.pallas import tpu as pltpu

F32 = jnp.float32
BF16 = jnp.bfloat16

D_MODEL = 2048
EPS = 1e-6
NEG = -1e30
LOG2E = 1.4426950408889634

MLA_HEADS = 8
MLA_Q_RANK = 384
MLA_KV_RANK = 128
MLA_NOPE = 64
MLA_ROPE = 32
MLA_V = 64
MLA_QK = MLA_NOPE + MLA_ROPE
ROPE_THETA = 10000.0

DIL_HEADS = 12
DIL_HD = 64
DIL_W = DIL_HEADS * DIL_HD
DIL_PATTERNS = ((128, 1), (512, 4), (2048, 16))
DIL_R = 64
N_BUCKETS = 32
BUCKET_MAX_DIST = 1024

ML_HEADS = 4
ML_HD = 192
ML_HP = 256
ML_W = ML_HEADS * ML_HD
ML_WP = ML_HEADS * ML_HP
ML_CONV = 5
ML_CHUNK = 128

N_GROUPS = 4
EXPERTS_PER_GROUP = 4
N_EXPERTS = 16
D_EXPERT = 512

LANES = 128
HALO = 8
MAIN_W = 6144
VMEM_LIMIT = 56 * 1024 * 1024


def _cparams(sem):
    return pltpu.CompilerParams(dimension_semantics=sem, vmem_limit_bytes=VMEM_LIMIT)


def _lane(shape):
    return lax.broadcasted_iota(jnp.int32, shape, len(shape) - 1)


def _split3(x):
    hi = x.astype(BF16)
    r1 = x - hi.astype(F32)
    mid = r1.astype(BF16)
    lo = (r1 - mid.astype(F32)).astype(BF16)
    return hi, mid, lo


def _dot(a, b):
    return jnp.dot(a, b, preferred_element_type=F32)


def _dot_nt(a, b):
    return lax.dot_general(a, b, (((1,), (1,)), ((), ())), preferred_element_type=F32)


def _dot_tn(a, b):
    return lax.dot_general(a, b, (((0,), (0,)), ((), ())), preferred_element_type=F32)


def _inproj_kernel(x_ref, g_ref, w_ref, wg_ref, o_ref, og_ref, xn_ref):
    @pl.when(pl.program_id(1) == 0)
    def _():
        x = x_ref[...]
        ms = jnp.mean(x * x, axis=-1, keepdims=True)
        xn = (x * lax.rsqrt(ms + EPS) * g_ref[...]).astype(BF16)
        xn_ref[...] = xn
        og_ref[...] = _dot(xn, wg_ref[...])

    o_ref[...] = _dot(xn_ref[...], w_ref[...]).astype(o_ref.dtype)


def _inproj(x, g, w_main, w_gate, tm=1024, tn=768):
    S, D = x.shape
    N = w_main.shape[1]
    return pl.pallas_call(
        _inproj_kernel,
        grid=(S // tm, N // tn),
        in_specs=[
            pl.BlockSpec((tm, D), lambda i, j: (i, 0)),
            pl.BlockSpec((1, D), lambda i, j: (0, 0)),
            pl.BlockSpec((D, tn), lambda i, j: (0, j)),
            pl.BlockSpec((D, LANES), lambda i, j: (0, 0)),
        ],
        out_specs=[
            pl.BlockSpec((tm, tn), lambda i, j: (i, j)),
            pl.BlockSpec((tm, LANES), lambda i, j: (i, 0)),
        ],
        out_shape=[jax.ShapeDtypeStruct((S, N), BF16), jax.ShapeDtypeStruct((S, LANES), F32)],
        scratch_shapes=[pltpu.VMEM((tm, D), BF16)],
        compiler_params=_cparams(("parallel", "arbitrary")),
        name="inproj",
    )(x, g, w_main, w_gate)


def _rope_tables(pos_ref, invf_ref):
    ang = pos_ref[...].astype(F32) * invf_ref[...]
    lane = _lane(ang.shape)
    cos = jnp.cos(ang)
    sin = jnp.sin(ang)
    rope_a = (lane >= MLA_NOPE) & (lane < MLA_NOPE + MLA_ROPE // 2)
    rope_b = (lane >= MLA_NOPE + MLA_ROPE // 2) & (lane < MLA_QK)
    c = jnp.where(lane < MLA_NOPE, 1.0, jnp.where(lane < MLA_QK, cos, 0.0))
    s1 = jnp.where(rope_a, -sin, 0.0)
    s2 = jnp.where(rope_b, sin, 0.0)
    return c, s1, s2


def _rope(x, tabs):
    c, s1, s2 = tabs
    half = MLA_ROPE // 2
    return x * c + pltpu.roll(x, LANES - half, 1) * s1 + pltpu.roll(x, half, 1) * s2


def _mla_prep_kernel(in_ref, pos_ref, invf_ref, gcq_ref, wq_ref, gckv_ref, wk_ref, wv_ref, vone_ref,
                     gq_ref, gk_ref, q_ref, k_ref, v_ref):
    xin = in_ref[...].astype(F32)
    cq = xin[:, :MLA_Q_RANK]
    ckv = xin[:, MLA_Q_RANK:MLA_Q_RANK + MLA_KV_RANK]
    kr_blk = xin[:, MLA_Q_RANK + MLA_KV_RANK:MLA_Q_RANK + MLA_KV_RANK + LANES]

    cqn = (cq * lax.rsqrt(jnp.mean(cq * cq, axis=-1, keepdims=True) + EPS) * gcq_ref[...]).astype(BF16)
    ckvn = (ckv * lax.rsqrt(jnp.mean(ckv * ckv, axis=-1, keepdims=True) + EPS) * gckv_ref[...]).astype(BF16)
    q = _dot(cqn, wq_ref[...])
    kn = _dot(ckvn, wk_ref[...])
    v = _dot(ckvn, wv_ref[...]) + vone_ref[...]
    v_ref[...] = v.astype(v_ref.dtype)

    tabs = _rope_tables(pos_ref, invf_ref)
    lane = _lane(kr_blk.shape)
    kr = jnp.where((lane >= MLA_NOPE) & (lane < MLA_QK), pltpu.roll(kr_blk, MLA_NOPE, 1), 0.0)
    kr = _rope(kr, tabs)

    q_scale = (MLA_QK ** -0.5) * LOG2E
    for h in range(MLA_HEADS):
        sl = slice(h * LANES, (h + 1) * LANES)
        qh = _rope(q[:, sl], tabs)
        ss = jnp.sum(qh * qh, axis=-1, keepdims=True) * (1.0 / MLA_QK)
        q_ref[:, sl] = (qh * lax.rsqrt(ss + EPS) * gq_ref[...] * q_scale).astype(q_ref.dtype)
        kh = kn[:, sl] + kr
        ss = jnp.sum(kh * kh, axis=-1, keepdims=True) * (1.0 / MLA_QK)
        k_ref[:, sl] = (kh * lax.rsqrt(ss + EPS) * gk_ref[...]).astype(k_ref.dtype)


def _mla_prep(main, pos, invf, gcq, wq, gckv, wk, wv, vone, gq, gk, tm=512):
    S = main.shape[0]
    HW = MLA_HEADS * LANES
    full = lambda shape: pl.BlockSpec(shape, lambda i: (0, 0))
    out = jax.ShapeDtypeStruct((S, HW), BF16)
    return pl.pallas_call(
        _mla_prep_kernel,
        grid=(S // tm,),
        in_specs=[
            pl.BlockSpec((tm, 768), lambda i: (i, 3)),
            pl.BlockSpec((tm, 1), lambda i: (i, 0)),
            full((1, LANES)), full((1, MLA_Q_RANK)), full((MLA_Q_RANK, HW)), full((1, MLA_KV_RANK)),
            full((MLA_KV_RANK, HW)), full((MLA_KV_RANK, HW)), full((1, HW)), full((1, LANES)), full((1, LANES)),
        ],
        out_specs=[pl.BlockSpec((tm, HW), lambda i: (i, 0))] * 3,
        out_shape=[out, out, out],
        compiler_params=_cparams(("parallel",)),
        name="mla_prep",
    )(main, pos, invf, gcq, wq, gckv, wk, wv, vone, gq, gk)


def _mla_attn_kernel(q_ref, k_ref, v_ref, o_ref, s_ref, *, tk, unroll):
    tq = q_ref.shape[0]
    S = k_ref.shape[0]
    nk = S // tk
    for hh in range(2):
        sl = slice(hh * LANES, (hh + 1) * LANES)
        q = q_ref[:, sl]

        def logits(c, mx, sl=sl, q=q, hh=hh):
            rows = pl.ds(pl.multiple_of(c * tk, tk), tk)
            s = _dot_nt(q, k_ref[rows, sl])
            s_ref[hh, c] = s
            for j in range(tk // LANES):
                mx = jnp.maximum(mx, s[:, j * LANES:(j + 1) * LANES])
            return mx

        mx = lax.fori_loop(0, nk, logits, jnp.full((tq, LANES), NEG, F32), unroll=unroll)
        m = jnp.max(mx, axis=-1, keepdims=True)

        def weigh(c, acc, sl=sl, m=m, hh=hh):
            rows = pl.ds(pl.multiple_of(c * tk, tk), tk)
            p = jnp.exp2(s_ref[hh, c] - m)
            return acc + _dot(p.astype(BF16), v_ref[rows, sl])

        acc = lax.fori_loop(0, nk, weigh, jnp.zeros((tq, LANES), F32), unroll=unroll)
        o = acc[:, :MLA_V] / acc[:, MLA_V:MLA_V + 1]
        o_ref[:, hh * MLA_V:(hh + 1) * MLA_V] = o.astype(o_ref.dtype)


def _mla_attn(q, k, v, tq=256, tk=512, unroll=16):
    S = q.shape[0]
    return pl.pallas_call(
        functools.partial(_mla_attn_kernel, tk=tk, unroll=unroll),
        grid=(MLA_HEADS // 2, S // tq),
        in_specs=[
            pl.BlockSpec((tq, 2 * LANES), lambda h, i: (i, h)),
            pl.BlockSpec((S, 2 * LANES), lambda h, i: (0, h)),
            pl.BlockSpec((S, 2 * LANES), lambda h, i: (0, h)),
        ],
        out_specs=pl.BlockSpec((tq, 2 * MLA_V), lambda h, i: (i, h)),
        out_shape=jax.ShapeDtypeStruct((S, MLA_HEADS * MLA_V), BF16),
        scratch_shapes=[pltpu.VMEM((2, S // tk, tq, tk), F32)],
        compiler_params=_cparams(("parallel", "parallel")),
        name="mla_attn",
    )(q, k, v)


def _dil_norm_kernel(q_ref, k_ref, gq_ref, gk_ref, qo_ref, ko_ref):
    def norm(x_ref, g_ref, o_ref, scale):
        for b in range(DIL_W // LANES):
            sl = slice(b * LANES, (b + 1) * LANES)
            x = x_ref[:, sl].astype(F32)
            lo = _lane(x.shape) < DIL_HD
            sq = x * x
            ss_lo = jnp.sum(jnp.where(lo, sq, 0.0), axis=-1, keepdims=True)
            ss_hi = jnp.sum(jnp.where(lo, 0.0, sq), axis=-1, keepdims=True)
            r = lax.rsqrt(jnp.where(lo, ss_lo, ss_hi) * (1.0 / DIL_HD) + EPS)
            o_ref[:, sl] = (x * r * g_ref[...] * scale).astype(o_ref.dtype)

    norm(q_ref, gq_ref, qo_ref, (DIL_HD ** -0.5) * LOG2E)
    norm(k_ref, gk_ref, ko_ref, 1.0)


def _dil_norm(main, gq, gk, tm=1024):
    S = main.shape[0]
    out = jax.ShapeDtypeStruct((S, DIL_W), BF16)
    return pl.pallas_call(
        _dil_norm_kernel,
        grid=(S // tm,),
        in_specs=[
            pl.BlockSpec((tm, DIL_W), lambda i: (i, 0)),
            pl.BlockSpec((tm, DIL_W), lambda i: (i, 1)),
            pl.BlockSpec((1, LANES), lambda i: (0, 0)),
            pl.BlockSpec((1, LANES), lambda i: (0, 0)),
        ],
        out_specs=[pl.BlockSpec((tm, DIL_W), lambda i: (i, 0))] * 2,
        out_shape=[out, out],
        compiler_params=_cparams(("parallel",)),
        name="dil_norm",
    )(main, main, gq, gk)


def _dil_band_kernel(q_ref, kp_ref, kc_ref, kn_ref, vp_ref, vc_ref, vn_ref, b_ref, o_ref, st_ref):
    ub = pl.program_id(1)
    nub = pl.num_programs(1)
    tq = q_ref.shape[0]
    win = tq + 2 * DIL_R
    col = _lane((1, win))
    prev_pen = jnp.where(ub > 0, 0.0, NEG)
    next_pen = jnp.where(ub < nub - 1, 0.0, NEG)
    edge = jnp.where(col < DIL_R, prev_pen, jnp.where(col >= DIL_R + tq, next_pen, 0.0))
    kwin = jnp.concatenate([kp_ref[...], kc_ref[...], kn_ref[...]], axis=0)
    vwin = jnp.concatenate([vp_ref[...], vc_ref[...], vn_ref[...]], axis=0)
    q = q_ref[...]
    st_lane = _lane((tq, LANES))
    st = jnp.zeros((tq, LANES), F32)
    for h in range(DIL_HEADS):
        sl = slice(h * DIL_HD, (h + 1) * DIL_HD)
        s = _dot_nt(q[:, sl], kwin[:, sl]) + b_ref[h] + edge
        m = jnp.max(s, axis=-1, keepdims=True)
        p = jnp.exp2(s - m)
        den = jnp.sum(p, axis=-1, keepdims=True)
        o = _dot(p.astype(BF16), vwin[:, sl]) / den
        o_ref[:, sl] = o.astype(o_ref.dtype)
        st = jnp.where(st_lane == h, m, jnp.where(st_lane == h + LANES // 2, den, st))
    st_ref[...] = st


def _dil_band(qn, kn, main, bias, d, tq=128):
    S = qn.shape[0]
    sub = S // d
    nub = sub // tq
    hb = tq // DIL_R
    mb = MAIN_W // DIL_W
    qv = qn.reshape(sub, d * DIL_W)
    kv = kn.reshape(sub, d * DIL_W)
    mv = main.reshape(sub, d * MAIN_W)
    last = sub // DIL_R - 1
    prev = lambda u: jnp.maximum(u * hb - 1, 0)
    nxt = lambda u: jnp.minimum(u * hb + hb, last)
    o, st = pl.pallas_call(
        _dil_band_kernel,
        grid=(d, nub),
        in_specs=[
            pl.BlockSpec((tq, DIL_W), lambda r, u: (u, r)),
            pl.BlockSpec((DIL_R, DIL_W), lambda r, u: (prev(u), r)),
            pl.BlockSpec((tq, DIL_W), lambda r, u: (u, r)),
            pl.BlockSpec((DIL_R, DIL_W), lambda r, u: (nxt(u), r)),
            pl.BlockSpec((DIL_R, DIL_W), lambda r, u: (prev(u), r * mb + 2)),
            pl.BlockSpec((tq, DIL_W), lambda r, u: (u, r * mb + 2)),
            pl.BlockSpec((DIL_R, DIL_W), lambda r, u: (nxt(u), r * mb + 2)),
            pl.BlockSpec((DIL_HEADS, tq, tq + 2 * DIL_R), lambda r, u: (0, 0, 0)),
        ],
        out_specs=[
            pl.BlockSpec((tq, DIL_W), lambda r, u: (u, r)),
            pl.BlockSpec((tq, LANES), lambda r, u: (u, r)),
        ],
        out_shape=[jax.ShapeDtypeStruct((sub, d * DIL_W), BF16), jax.ShapeDtypeStruct((sub, d * LANES), F32)],
        compiler_params=_cparams(("parallel", "parallel")),
        name=f"dil_band{d}",
    )(qv, kv, kv, kv, mv, mv, mv, bias)
    return o.reshape(S, DIL_W), st.reshape(S, LANES)


def _log_sigmoid(x):
    return jnp.minimum(x, 0.0) - jnp.log(1.0 + jnp.exp(-jnp.abs(x)))


def _ml_prep_kernel(up_ref, uc_ref, un_ref, cw_ref, cb_ref, wqk_ref, g_ref, gb_ref,
                    q_ref, k_ref, gcol_ref, grow_ref):
    i = pl.program_id(0)
    tm = uc_ref.shape[0]
    pm = jnp.where(i > 0, 1.0, 0.0)
    nm = jnp.where(i < pl.num_programs(0) - 1, 1.0, 0.0)
    xc = jnp.concatenate([up_ref[...].astype(F32) * pm, uc_ref[...].astype(F32), un_ref[...].astype(F32) * nm],
                         axis=0)
    acc = jnp.zeros((tm, ML_WP), F32) + cb_ref[...]
    for j in range(ML_CONV):
        off = HALO - ML_CONV // 2 + j
        acc = acc + xc[off:off + tm] * cw_ref[j:j + 1, :]
    u = (acc * jax.nn.sigmoid(acc)).astype(BF16)
    qk = _dot(u, wqk_ref[...])
    q_ref[...] = qk[:, :ML_WP].astype(q_ref.dtype)
    k_ref[...] = qk[:, ML_WP:].astype(k_ref.dtype)

    g = g_ref[...] + gb_ref[...]
    lane = _lane(g.shape)
    is_f = (lane % 8) >= ML_HEADS
    gcol = jnp.where(lane < 4 * ML_HEADS, jnp.where(is_f, _log_sigmoid(g), g), 0.0)
    gcol_ref[...] = gcol
    grow_ref[...] = gcol.T


def _ml_prep(main, gates, cw, cb, wqk, gb, tm=512):
    S = main.shape[0]
    hb = tm // HALO
    last = S // HALO - 1
    full = lambda shape: pl.BlockSpec(shape, lambda i: (0, 0))
    return pl.pallas_call(
        _ml_prep_kernel,
        grid=(S // tm,),
        in_specs=[
            pl.BlockSpec((HALO, ML_WP), lambda i: (jnp.maximum(i * hb - 1, 0), 3)),
            pl.BlockSpec((tm, ML_WP), lambda i: (i, 3)),
            pl.BlockSpec((HALO, ML_WP), lambda i: (jnp.minimum(i * hb + hb, last), 3)),
            full((HALO, ML_WP)), full((1, ML_WP)), full((ML_WP, 2 * ML_WP)),
            pl.BlockSpec((tm, LANES), lambda i: (i, 0)),
            full((1, LANES)),
        ],
        out_specs=[
            pl.BlockSpec((tm, ML_WP), lambda i: (i, 0)),
            pl.BlockSpec((tm, ML_WP), lambda i: (i, 0)),
            pl.BlockSpec((tm, LANES), lambda i: (i, 0)),
            pl.BlockSpec((LANES, tm), lambda i: (0, i)),
        ],
        out_shape=[
            jax.ShapeDtypeStruct((S, ML_WP), BF16), jax.ShapeDtypeStruct((S, ML_WP), BF16),
            jax.ShapeDtypeStruct((S, LANES), F32), jax.ShapeDtypeStruct((LANES, S), F32),
        ],
        compiler_params=_cparams(("parallel",)),
        name="ml_prep",
    )(main, main, main, cw, cb, wqk, gates, gb)


def _ml_scan_kernel(qf_ref, kf_ref, vf_ref, gcf_ref, grf_ref, qb_ref, kb_ref, vb_ref, gcb_ref, grb_ref,
                    hf_ref, hb_ref, c_ref, n_ref, m_ref):
    L = ML_CHUNK

    @pl.when(pl.program_id(0) == 0)
    def _():
        c_ref[...] = jnp.zeros_like(c_ref)
        n_ref[...] = jnp.zeros_like(n_ref)
        m_ref[...] = jnp.zeros_like(m_ref)

    row = lax.broadcasted_iota(jnp.int32, (L, L), 0)
    colm = lax.broadcasted_iota(jnp.int32, (L, L), 1)
    lower = colm <= row
    upper = colm >= row
    t_lower = jnp.where(lower, 1.0, 0.0).astype(BF16)
    t_upper = jnp.where(upper, 1.0, 0.0).astype(BF16)

    def tri_left(t, x):
        hi, mid, lo = _split3(x)
        return _dot(t, hi) + _dot(t, mid) + _dot(t, lo)

    def tri_right(x, t):
        hi, mid, lo = _split3(x)
        return _dot(hi, t) + _dot(mid, t) + _dot(lo, t)

    dirs = (
        (qf_ref, kf_ref, vf_ref, gcf_ref, grf_ref, hf_ref, t_lower, t_upper, lower, L - 1),
        (qb_ref, kb_ref, vb_ref, gcb_ref, grb_ref, hb_ref, t_upper, t_lower, upper, 0),
    )
    for dd, (q_ref, k_ref, v_ref, gc_ref, gr_ref, h_ref, t_col, t_row, mask, last) in enumerate(dirs):
        gcol = gc_ref[...]
        grow = gr_ref[...]
        bcol = tri_left(t_col, gcol)
        brow = tri_right(grow, t_row)
        for h in range(ML_HEADS):
            ch = dd * ML_HEADS + h
            li = dd * 2 * ML_HEADS + h
            lf = li + ML_HEADS
            sl = slice(h * ML_HP, (h + 1) * ML_HP)
            q = q_ref[:, sl]
            k = k_ref[:, sl]
            v = v_ref[:, sl]
            b_c = bcol[:, lf:lf + 1]
            b_r = brow[lf:lf + 1, :]
            li_c = gcol[:, li:li + 1]
            li_r = grow[li:li + 1, :]
            m_prev = m_ref[ch:ch + 1, 0:1]
            c_prev = c_ref[ch]
            n_prev = n_ref[ch]

            logw = jnp.where(mask, b_c - b_r + li_r, NEG)
            inter = b_c + m_prev
            m = jnp.maximum(jnp.max(logw, axis=-1, keepdims=True), inter)
            a = jnp.exp(logw - m) * _dot_nt(q, k)
            decay = jnp.exp(inter - m)
            num = _dot(a.astype(BF16), v) + decay * _dot(q, c_prev.astype(BF16))
            qn = jnp.sum(q.astype(F32) * n_prev, axis=-1, keepdims=True)
            den = jnp.sum(a, axis=-1, keepdims=True) + decay * qn
            h_ref[:, sl] = num / jnp.maximum(jnp.abs(den), jnp.exp(-m))

            m_new = m[last:last + 1, :]
            b_last = b_c[last:last + 1, :]
            g = jnp.exp(b_last - b_c + li_c - m_new)
            cd = jnp.exp(b_last + m_prev - m_new)
            kg = k.astype(F32) * g
            c_ref[ch] = cd * c_prev + _dot_tn(kg.astype(BF16), v)
            n_ref[ch] = cd * n_prev + jnp.sum(kg, axis=0, keepdims=True)
            m_ref[ch:ch + 1, :] = jnp.broadcast_to(m_new, (1, LANES))


def _ml_scan(q, k, main, gcol, grow):
    S = q.shape[0]
    L = ML_CHUNK
    nc = S // L
    fwd = lambda j: (j, 0)
    bwd = lambda j: (nc - 1 - j, 0)
    blk = lambda f: pl.BlockSpec((L, ML_WP), f)
    vblk = lambda f: pl.BlockSpec((L, ML_WP), lambda j: (f(j)[0], 4))
    out = jax.ShapeDtypeStruct((S, ML_WP), F32)
    return pl.pallas_call(
        _ml_scan_kernel,
        grid=(nc,),
        in_specs=[
            blk(fwd), blk(fwd), vblk(fwd), pl.BlockSpec((L, LANES), fwd), pl.BlockSpec((LANES, L), lambda j: (0, j)),
            blk(bwd), blk(bwd), vblk(bwd), pl.BlockSpec((L, LANES), bwd),
            pl.BlockSpec((LANES, L), lambda j: (0, nc - 1 - j)),
        ],
        out_specs=[blk(fwd), blk(bwd)],
        out_shape=[out, out],
        scratch_shapes=[
            pltpu.VMEM((2 * ML_HEADS, ML_HP, ML_HP), F32),
            pltpu.VMEM((2 * ML_HEADS, 1, ML_HP), F32),
            pltpu.VMEM((2 * ML_HEADS, LANES), F32),
        ],
        compiler_params=_cparams(("arbitrary",)),
        name="ml_scan",
    )(q, k, main, gcol, grow, q, k, main, gcol, grow)


def _outproj_kernel(x_ref, ymla_ref, o1_ref, o2_ref, o3_ref, s1_ref, s2_ref, s3_ref, hf_ref, hb_ref, mo_ref,
                    gout_ref, e_ref, w1_ref, w2_ref, w3_ref, out_ref):
    sts = [s1_ref[...], s2_ref[...], s3_ref[...]]
    m_all = jnp.maximum(jnp.maximum(sts[0], sts[1]), sts[2])
    ws = [jnp.exp2(st - m_all) * pltpu.roll(st, LANES // 2, 1) for st in sts]
    inv = 1.0 / (ws[0] + ws[1] + ws[2])
    lane = _lane(inv.shape)
    ydil = jnp.zeros(o1_ref.shape, F32)
    for w, o_ref in zip(ws, (o1_ref, o2_ref, o3_ref)):
        alpha = jnp.where(lane < DIL_HEADS, w * inv, 0.0)
        hi, mid, lo = _split3(alpha)
        e = e_ref[...]
        ydil = ydil + (_dot(hi, e) + _dot(mid, e) + _dot(lo, e)) * o_ref[...].astype(F32)

    hs = hf_ref[...] + hb_ref[...]
    yml = []
    for h in range(ML_HEADS):
        sl = slice(h * ML_HP, (h + 1) * ML_HP)
        hh = hs[:, sl]
        ss = jnp.sum(hh * hh, axis=-1, keepdims=True) * (1.0 / ML_HD)
        yml.append(hh * lax.rsqrt(ss + EPS) * gout_ref[:, sl] * jax.nn.sigmoid(mo_ref[:, sl].astype(F32)))
    yml = jnp.concatenate(yml, axis=-1)

    out_ref[...] = (x_ref[...] + _dot(ymla_ref[...], w1_ref[...]) + _dot(ydil.astype(BF16), w2_ref[...])
                    + _dot(yml.astype(BF16), w3_ref[...]))


def _outproj(x, ymla, dil_o, dil_st, hf, hb, main, gout, e, w1, w2, w3, tm=512):
    S, D = x.shape
    row = lambda w: pl.BlockSpec((tm, w), lambda i: (i, 0))
    full = lambda shape: pl.BlockSpec(shape, lambda i: (0, 0))
    return pl.pallas_call(
        _outproj_kernel,
        grid=(S // tm,),
        in_specs=[
            row(D), row(MLA_HEADS * MLA_V), row(DIL_W), row(DIL_W), row(DIL_W), row(LANES), row(LANES), row(LANES),
            row(ML_WP), row(ML_WP), pl.BlockSpec((tm, ML_WP), lambda i: (i, 5)),
            full((1, ML_WP)), full((LANES, DIL_W)), full((MLA_HEADS * MLA_V, D)), full((DIL_W, D)), full((ML_WP, D)),
        ],
        out_specs=row(D),
        out_shape=jax.ShapeDtypeStruct((S, D), F32),
        compiler_params=_cparams(("parallel",)),
        name="outproj",
    )(x, ymla, *dil_o, *dil_st, hf, hb, main, gout, e, w1, w2, w3)


def _router_kernel(x_ref, g_ref, wr_ref, br_ref, xn_ref, comb_ref):
    x = x_ref[...]
    xn = x * lax.rsqrt(jnp.mean(x * x, axis=-1, keepdims=True) + EPS) * g_ref[...]
    xn_ref[...] = xn.astype(xn_ref.dtype)
    x1, x2, x3 = _split3(xn)
    w1 = wr_ref[0]
    w2 = wr_ref[1]
    w3 = wr_ref[2]
    logits = (_dot(x1, w1) + (_dot(x1, w2) + _dot(x2, w1)) + (_dot(x1, w3) + _dot(x2, w2) + _dot(x3, w1))
              + br_ref[...])
    lane = _lane(logits.shape)
    big = jnp.int32(4 * LANES)

    def first_max(mask):
        v = jnp.max(jnp.where(mask, logits, NEG), axis=-1, keepdims=True)
        idx = jnp.min(jnp.where(mask & (logits == v), lane, big), axis=-1, keepdims=True)
        return v, idx

    gmask = lane < N_GROUPS
    gmax, grp = first_max(gmask)
    p_grp = 1.0 / jnp.sum(jnp.where(gmask, jnp.exp(logits - gmax), 0.0), axis=-1, keepdims=True)
    e_lo = N_GROUPS + grp * EXPERTS_PER_GROUP
    emask = (lane >= e_lo) & (lane < e_lo + EXPERTS_PER_GROUP)
    v1, i1 = first_max(emask)
    v2, i2 = first_max(emask & (lane != i1))
    t = jnp.exp(v2 - v1)
    w_1 = p_grp / (1.0 + t)
    w_2 = p_grp * t / (1.0 + t)
    comb_ref[...] = jnp.where(lane == i1 - N_GROUPS, w_1, 0.0) + jnp.where(lane == i2 - N_GROUPS, w_2, 0.0)


def _router(x, g, wr, br, tm=512):
    S, D = x.shape
    return pl.pallas_call(
        _router_kernel,
        grid=(S // tm,),
        in_specs=[
            pl.BlockSpec((tm, D), lambda i: (i, 0)),
            pl.BlockSpec((1, D), lambda i: (0, 0)),
            pl.BlockSpec((3, D, LANES), lambda i: (0, 0, 0)),
            pl.BlockSpec((1, LANES), lambda i: (0, 0)),
        ],
        out_specs=[pl.BlockSpec((tm, D), lambda i: (i, 0)), pl.BlockSpec((tm, LANES), lambda i: (i, 0))],
        out_shape=[jax.ShapeDtypeStruct((S, D), BF16), jax.ShapeDtypeStruct((S, LANES), F32)],
        compiler_params=_cparams(("parallel",)),
        name="router",
    )(x, g, wr, br)


def _moe_kernel(x_ref, xn_ref, comb_ref, wg_ref, wu_ref, wd_ref, out_ref):
    e = pl.program_id(1)

    @pl.when(e == 0)
    def _():
        out_ref[...] = x_ref[...]

    xn = xn_ref[...]
    comb = comb_ref[...]
    c = jnp.sum(jnp.where(_lane(comb.shape) == e, comb, 0.0), axis=-1, keepdims=True)
    gate = _dot(xn, wg_ref[0])
    up = _dot(xn, wu_ref[0])
    hid = gate * jax.nn.sigmoid(gate) * up * c
    out_ref[...] += _dot(hid.astype(BF16), wd_ref[0])


def _moe(x, xn, comb, wg, wu, wd, tm=512):
    S, D = x.shape
    return pl.pallas_call(
        _moe_kernel,
        grid=(S // tm, N_EXPERTS),
        in_specs=[
            pl.BlockSpec((tm, D), lambda i, e: (i, 0)),
            pl.BlockSpec((tm, D), lambda i, e: (i, 0)),
            pl.BlockSpec((tm, LANES), lambda i, e: (i, 0)),
            pl.BlockSpec((1, D, D_EXPERT), lambda i, e: (e, 0, 0)),
            pl.BlockSpec((1, D, D_EXPERT), lambda i, e: (e, 0, 0)),
            pl.BlockSpec((1, D_EXPERT, D), lambda i, e: (e, 0, 0)),
        ],
        out_specs=pl.BlockSpec((tm, D), lambda i, e: (i, 0)),
        out_shape=jax.ShapeDtypeStruct((S, D), F32),
        compiler_params=_cparams(("parallel", "arbitrary")),
        name="moe",
    )(x, xn, comb, wg, wu, wd)


def _pad_cols(w, width):
    return jnp.pad(w, ((0, 0), (0, width - w.shape[1])))


def _pad_heads(w):
    lead = w.shape[:-1]
    w = w.reshape(*lead, ML_HEADS, ML_HD)
    w = jnp.pad(w, [(0, 0)] * len(lead) + [(0, 0), (0, ML_HP - ML_HD)])
    return w.reshape(*lead, ML_WP)


def _t5_buckets(rel):
    half = N_BUCKETS // 2
    max_exact = half // 2
    n = np.abs(rel)
    large = max_exact + (np.log(np.maximum(n, 1) / max_exact) / np.log(BUCKET_MAX_DIST / max_exact)
                         * (half - max_exact)).astype(np.int32)
    large = np.minimum(large, half - 1)
    return (rel > 0).astype(np.int32) * half + np.where(n < max_exact, n, large).astype(np.int32)


def _band_bias(rel_bias, d, tq=128):
    win = tq + 2 * DIL_R
    j = np.arange(win)[None, :] - DIL_R - np.arange(tq)[:, None]
    inside = np.abs(j) <= DIL_R
    buckets = _t5_buckets(np.where(inside, j, 0) * d)
    b = rel_bias.astype(F32)[jnp.asarray(buckets)] * LOG2E
    b = jnp.where(jnp.asarray(inside)[..., None], b, NEG)
    return jnp.transpose(b, (2, 0, 1))


def _layer_weights(l, p):
    w_in = p["w_in"][l]
    off = np.cumsum((0, MLA_Q_RANK, MLA_KV_RANK, MLA_ROPE, DIL_W, DIL_W, DIL_W, ML_W, ML_W, ML_W, 4 * ML_HEADS))
    col = lambda a, b: w_in[:, off[a]:off[b]]
    w_main = jnp.concatenate([
        col(3, 6),
        _pad_cols(col(0, 3), DIL_W),
        _pad_cols(col(6, 7), ML_WP),
        _pad_heads(col(7, 8)), _pad_heads(col(8, 9)),
    ], axis=1).astype(BF16)
    w_gate = _pad_cols(col(9, 10), LANES).astype(BF16)

    wq = p["mla_w_uq"][l].reshape(MLA_Q_RANK, MLA_HEADS, MLA_QK)
    wq = jnp.pad(wq, ((0, 0), (0, 0), (0, LANES - MLA_QK))).reshape(MLA_Q_RANK, MLA_HEADS * LANES).astype(BF16)
    wkv = p["mla_w_ukv"][l].reshape(MLA_KV_RANK, MLA_HEADS, MLA_NOPE + MLA_V)
    padh = lambda w: jnp.pad(w, ((0, 0), (0, 0), (0, LANES - w.shape[-1]))).reshape(
        MLA_KV_RANK, MLA_HEADS * LANES).astype(BF16)
    wk = padh(wkv[..., :MLA_NOPE])
    wv = padh(wkv[..., MLA_NOPE:])
    vone = jnp.tile((jnp.arange(LANES) == MLA_V).astype(F32), MLA_HEADS)[None, :]
    lane_pad = lambda g: jnp.pad(g, (0, LANES - g.shape[0]))[None, :]

    def block_diag(w):
        out = jnp.zeros((ML_WP, ML_WP), F32)
        for h in range(ML_HEADS):
            out = out.at[h * ML_HD:(h + 1) * ML_HD, h * ML_HP:h * ML_HP + ML_HD].set(w[h])
        return out
    wqk = jnp.concatenate([block_diag(p["ml_w_q"][l]), block_diag(p["ml_w_k"][l]) * (ML_HD ** -0.5)],
                          axis=1).astype(BF16)
    cw = jnp.pad(p["ml_conv_w"][l], ((0, HALO - ML_CONV), (0, ML_WP - ML_W)))
    cb = _pad_cols(p["ml_conv_b"][l][None, :], ML_WP)
    gb = _pad_cols(p["ml_gate_bias"][l].reshape(1, 4 * ML_HEADS), LANES)

    w_out = p["w_out"][l]
    n_mla = MLA_HEADS * MLA_V
    w3 = w_out[n_mla + DIL_W:].reshape(ML_HEADS, ML_HD, D_MODEL)
    w3 = jnp.pad(w3, ((0, 0), (0, ML_HP - ML_HD), (0, 0))).reshape(ML_WP, D_MODEL)

    wr = jnp.concatenate([p["router_group_w"][l], p["router_expert_w"][l]], axis=1)
    wr = jnp.stack(_split3(_pad_cols(wr, LANES)))
    br = _pad_cols(jnp.concatenate([p["router_group_b"][l], p["router_expert_b"][l]])[None, :], LANES)
    return dict(
        norm_mix=p["norm_mix"][l][None, :], w_main=w_main, w_gate=w_gate,
        gcq=p["mla_norm_cq"][l][None, :], wq=wq, gckv=p["mla_norm_ckv"][l][None, :], wk=wk, wv=wv, vone=vone,
        gq=lane_pad(p["mla_q_norm"][l]), gk=lane_pad(p["mla_k_norm"][l]),
        dgq=jnp.tile(p["dil_q_norm"][l], 2)[None, :], dgk=jnp.tile(p["dil_k_norm"][l], 2)[None, :],
        cw=cw, cb=cb, wqk=wqk, gb=gb, gout=_pad_heads(p["ml_out_norm"][l][None, :]),
        w1=w_out[:n_mla].astype(BF16), w2=w_out[n_mla:n_mla + DIL_W].astype(BF16), w3=w3.astype(BF16),
        norm_ffn=p["norm_ffn"][l][None, :], wr=wr, br=br,
        moe_wg=p["moe_w_gate"][l].astype(BF16), moe_wu=p["moe_w_up"][l].astype(BF16),
        moe_wd=p["moe_w_down"][l].astype(BF16),
    )


def kernel(x, positions, rel_bias, norm_mix, w_in, mla_norm_cq, mla_w_uq, mla_norm_ckv, mla_w_ukv, mla_q_norm, mla_k_norm, dil_q_norm, dil_k_norm, ml_conv_w, ml_conv_b, ml_w_q, ml_w_k, ml_gate_bias, ml_out_norm, w_out, norm_ffn, router_group_w, router_group_b, router_expert_w, router_expert_b, moe_w_gate, moe_w_up, moe_w_down):
    params = dict(norm_mix=norm_mix, w_in=w_in, mla_norm_cq=mla_norm_cq, mla_w_uq=mla_w_uq,
                  mla_norm_ckv=mla_norm_ckv, mla_w_ukv=mla_w_ukv, mla_q_norm=mla_q_norm, mla_k_norm=mla_k_norm,
                  dil_q_norm=dil_q_norm, dil_k_norm=dil_k_norm, ml_conv_w=ml_conv_w, ml_conv_b=ml_conv_b,
                  ml_w_q=ml_w_q, ml_w_k=ml_w_k, ml_gate_bias=ml_gate_bias, ml_out_norm=ml_out_norm, w_out=w_out,
                  norm_ffn=norm_ffn, router_group_w=router_group_w, router_group_b=router_group_b,
                  router_expert_w=router_expert_w, router_expert_b=router_expert_b, moe_w_gate=moe_w_gate,
                  moe_w_up=moe_w_up, moe_w_down=moe_w_down)
    B, S, D = x.shape
    assert B == 1 and D == D_MODEL
    depth = w_in.shape[0]
    xs = x.reshape(S, D)
    pos = positions.reshape(S, 1)

    half = MLA_ROPE // 2
    inv_freq = ROPE_THETA ** (-jnp.arange(half, dtype=F32) / half)
    invf = jnp.zeros((LANES,), F32).at[MLA_NOPE:MLA_NOPE + half].set(inv_freq)
    invf = invf.at[MLA_NOPE + half:MLA_QK].set(inv_freq)[None, :]
    biases = [_band_bias(rel_bias, d) for _, d in DIL_PATTERNS]
    expand = (jnp.arange(LANES)[:, None] == (jnp.arange(DIL_W) // DIL_HD)[None, :]).astype(BF16)

    for l in range(depth):
        w = _layer_weights(l, params)
        main, gates = _inproj(xs, w["norm_mix"], w["w_main"], w["w_gate"])
        q, k, v = _mla_prep(main, pos, invf, w["gcq"], w["wq"], w["gckv"], w["wk"], w["wv"], w["vone"],
                            w["gq"], w["gk"])
        ymla = _mla_attn(q, k, v)
        qn, kn = _dil_norm(main, w["dgq"], w["dgk"])
        dil = [_dil_band(qn, kn, main, b, d) for b, (_, d) in zip(biases, DIL_PATTERNS)]
        mq, mk, gcol, grow = _ml_prep(main, gates, w["cw"], w["cb"], w["wqk"], w["gb"])
        hf, hb = _ml_scan(mq, mk, main, gcol, grow)
        xs = _outproj(xs, ymla, [o for o, _ in dil], [s for _, s in dil], hf, hb, main, w["gout"], expand,
                      w["w1"], w["w2"], w["w3"])
        xn, comb = _router(xs, w["norm_ffn"], w["wr"], w["br"])
        xs = _moe(xs, xn, comb, w["moe_wg"], w["moe_wu"], w["moe_wd"])
    return xs.reshape(B, S, D)
```

```python
import functools
import math

import numpy as np
import jax
import jax.numpy as jnp
from jax import lax
from jax.experimental import pallas as pl
from jax.experimental.pallas import tpu as pltpu

F32 = jnp.float32
BF16 = jnp.bfloat16

D_MODEL = 2048
EPS = 1e-6
NEG = -1e30
LOG2E = 1.4426950408889634

MLA_HEADS = 8
MLA_Q_RANK = 384
MLA_KV_RANK = 128
MLA_NOPE = 64
MLA_ROPE = 32
MLA_V = 64
MLA_QK = MLA_NOPE + MLA_ROPE
ROPE_THETA = 10000.0

DIL_HEADS = 12
DIL_HD = 64
DIL_W = DIL_HEADS * DIL_HD
DIL_PATTERNS = ((128, 1), (512, 4), (2048, 16))
DIL_R = 64
N_BUCKETS = 32
BUCKET_MAX_DIST = 1024

ML_HEADS = 4
ML_HD = 192
ML_HP = 256
ML_W = ML_HEADS * ML_HD
ML_WP = ML_HEADS * ML_HP
ML_CONV = 5
ML_CHUNK = 128

N_GROUPS = 4
EXPERTS_PER_GROUP = 4
N_EXPERTS = 16
D_EXPERT = 512

LANES = 128
HALO = 8
MAIN_W = 6144
VMEM_LIMIT = 56 * 1024 * 1024


def _cparams(sem):
    return pltpu.CompilerParams(dimension_semantics=sem, vmem_limit_bytes=VMEM_LIMIT)


def _lane(shape):
    return lax.broadcasted_iota(jnp.int32, shape, len(shape) - 1)


def _split3(x):
    hi = x.astype(BF16)
    r1 = x - hi.astype(F32)
    mid = r1.astype(BF16)
    lo = (r1 - mid.astype(F32)).astype(BF16)
    return hi, mid, lo


def _dot(a, b):
    return jnp.dot(a, b, preferred_element_type=F32)


def _dot_nt(a, b):
    return lax.dot_general(a, b, (((1,), (1,)), ((), ())), preferred_element_type=F32)


def _dot_tn(a, b):
    return lax.dot_general(a, b, (((0,), (0,)), ((), ())), preferred_element_type=F32)


def _inproj_kernel(x_ref, g_ref, w_ref, wg_ref, o_ref, og_ref, xn_ref):
    @pl.when(pl.program_id(1) == 0)
    def _():
        x = x_ref[...]
        ms = jnp.mean(x * x, axis=-1, keepdims=True)
        xn = (x * lax.rsqrt(ms + EPS) * g_ref[...]).astype(BF16)
        xn_ref[...] = xn
        og_ref[...] = _dot(xn, wg_ref[...])

    o_ref[...] = _dot(xn_ref[...], w_ref[...]).astype(o_ref.dtype)


def _inproj(x, g, w_main, w_gate, tm=1024, tn=768):
    S, D = x.shape
    N = w_main.shape[1]
    return pl.pallas_call(
        _inproj_kernel,
        grid=(S // tm, N // tn),
        in_specs=[
            pl.BlockSpec((tm, D), lambda i, j: (i, 0)),
            pl.BlockSpec((1, D), lambda i, j: (0, 0)),
            pl.BlockSpec((D, tn), lambda i, j: (0, j)),
            pl.BlockSpec((D, LANES), lambda i, j: (0, 0)),
        ],
        out_specs=[
            pl.BlockSpec((tm, tn), lambda i, j: (i, j)),
            pl.BlockSpec((tm, LANES), lambda i, j: (i, 0)),
        ],
        out_shape=[jax.ShapeDtypeStruct((S, N), BF16), jax.ShapeDtypeStruct((S, LANES), F32)],
        scratch_shapes=[pltpu.VMEM((tm, D), BF16)],
        compiler_params=_cparams(("parallel", "arbitrary")),
        name="inproj",
    )(x, g, w_main, w_gate)


def _rope_tables(pos_ref, invf_ref):
    ang = pos_ref[...].astype(F32) * invf_ref[...]
    lane = _lane(ang.shape)
    cos = jnp.cos(ang)
    sin = jnp.sin(ang)
    rope_a = (lane >= MLA_NOPE) & (lane < MLA_NOPE + MLA_ROPE // 2)
    rope_b = (lane >= MLA_NOPE + MLA_ROPE // 2) & (lane < MLA_QK)
    c = jnp.where(lane < MLA_NOPE, 1.0, jnp.where(lane < MLA_QK, cos, 0.0))
    s1 = jnp.where(rope_a, -sin, 0.0)
    s2 = jnp.where(rope_b, sin, 0.0)
    return c, s1, s2


def _rope(x, tabs):
    c, s1, s2 = tabs
    half = MLA_ROPE // 2
    return x * c + pltpu.roll(x, LANES - half, 1) * s1 + pltpu.roll(x, half, 1) * s2


def _mla_prep_kernel(in_ref, pos_ref, invf_ref, gcq_ref, wq_ref, gckv_ref, wk_ref, wv_ref, vone_ref,
                     gq_ref, gk_ref, q_ref, k_ref, v_ref):
    xin = in_ref[...].astype(F32)
    cq = xin[:, :MLA_Q_RANK]
    ckv = xin[:, MLA_Q_RANK:MLA_Q_RANK + MLA_KV_RANK]
    kr_blk = xin[:, MLA_Q_RANK + MLA_KV_RANK:MLA_Q_RANK + MLA_KV_RANK + LANES]

    cqn = (cq * lax.rsqrt(jnp.mean(cq * cq, axis=-1, keepdims=True) + EPS) * gcq_ref[...]).astype(BF16)
    ckvn = (ckv * lax.rsqrt(jnp.mean(ckv * ckv, axis=-1, keepdims=True) + EPS) * gckv_ref[...]).astype(BF16)
    q = _dot(cqn, wq_ref[...])
    kn = _dot(ckvn, wk_ref[...])
    v = _dot(ckvn, wv_ref[...]) + vone_ref[...]
    v_ref[...] = v.astype(v_ref.dtype)

    tabs = _rope_tables(pos_ref, invf_ref)
    lane = _lane(kr_blk.shape)
    kr = jnp.where((lane >= MLA_NOPE) & (lane < MLA_QK), pltpu.roll(kr_blk, MLA_NOPE, 1), 0.0)
    kr = _rope(kr, tabs)

    q_scale = (MLA_QK ** -0.5) * LOG2E
    for h in range(MLA_HEADS):
        sl = slice(h * LANES, (h + 1) * LANES)
        qh = _rope(q[:, sl], tabs)
        ss = jnp.sum(qh * qh, axis=-1, keepdims=True) * (1.0 / MLA_QK)
        q_ref[:, sl] = (qh * lax.rsqrt(ss + EPS) * gq_ref[...] * q_scale).astype(q_ref.dtype)
        kh = kn[:, sl] + kr
        ss = jnp.sum(kh * kh, axis=-1, keepdims=True) * (1.0 / MLA_QK)
        k_ref[:, sl] = (kh * lax.rsqrt(ss + EPS) * gk_ref[...]).astype(k_ref.dtype)


def _mla_prep(main, pos, invf, gcq, wq, gckv, wk, wv, vone, gq, gk, tm=512):
    S = main.shape[0]
    HW = MLA_HEADS * LANES
    full = lambda shape: pl.BlockSpec(shape, lambda i: (0, 0))
    out = jax.ShapeDtypeStruct((S, HW), BF16)
    return pl.pallas_call(
        _mla_prep_kernel,
        grid=(S // tm,),
        in_specs=[
            pl.BlockSpec((tm, 768), lambda i: (i, 3)),
            pl.BlockSpec((tm, 1), lambda i: (i, 0)),
            full((1, LANES)), full((1, MLA_Q_RANK)), full((MLA_Q_RANK, HW)), full((1, MLA_KV_RANK)),
            full((MLA_KV_RANK, HW)), full((MLA_KV_RANK, HW)), full((1, HW)), full((1, LANES)), full((1, LANES)),
        ],
        out_specs=[pl.BlockSpec((tm, HW), lambda i: (i, 0))] * 3,
        out_shape=[out, out, out],
        compiler_params=_cparams(("parallel",)),
        name="mla_prep",
    )(main, pos, invf, gcq, wq, gckv, wk, wv, vone, gq, gk)


def _mla_attn_kernel(q_ref, k_ref, v_ref, o_ref, s_ref, *, tk, unroll):
    tq = q_ref.shape[0]
    S = k_ref.shape[0]
    nk = S // tk
    for hh in range(2):
        sl = slice(hh * LANES, (hh + 1) * LANES)
        q = q_ref[:, sl]

        def logits(c, mx, sl=sl, q=q, hh=hh):
            rows = pl.ds(pl.multiple_of(c * tk, tk), tk)
            s = _dot_nt(q, k_ref[rows, sl])
            s_ref[hh, c] = s
            for j in range(tk // LANES):
                mx = jnp.maximum(mx, s[:, j * LANES:(j + 1) * LANES])
            return mx

        mx = lax.fori_loop(0, nk, logits, jnp.full((tq, LANES), NEG, F32), unroll=unroll)
        m = jnp.max(mx, axis=-1, keepdims=True)

        def weigh(c, acc, sl=sl, m=m, hh=hh):
            rows = pl.ds(pl.multiple_of(c * tk, tk), tk)
            p = jnp.exp2(s_ref[hh, c] - m)
            return acc + _dot(p.astype(BF16), v_ref[rows, sl])

        acc = lax.fori_loop(0, nk, weigh, jnp.zeros((tq, LANES), F32), unroll=unroll)
        o = acc[:, :MLA_V] / acc[:, MLA_V:MLA_V + 1]
        o_ref[:, hh * MLA_V:(hh + 1) * MLA_V] = o.astype(o_ref.dtype)


def _mla_attn(q, k, v, tq=256, tk=512, unroll=16):
    S = q.shape[0]
    return pl.pallas_call(
        functools.partial(_mla_attn_kernel, tk=tk, unroll=unroll),
        grid=(MLA_HEADS // 2, S // tq),
        in_specs=[
            pl.BlockSpec((tq, 2 * LANES), lambda h, i: (i, h)),
            pl.BlockSpec((S, 2 * LANES), lambda h, i: (0, h)),
            pl.BlockSpec((S, 2 * LANES), lambda h, i: (0, h)),
        ],
        out_specs=pl.BlockSpec((tq, 2 * MLA_V), lambda h, i: (i, h)),
        out_shape=jax.ShapeDtypeStruct((S, MLA_HEADS * MLA_V), BF16),
        scratch_shapes=[pltpu.VMEM((2, S // tk, tq, tk), F32)],
        compiler_params=_cparams(("parallel", "parallel")),
        name="mla_attn",
    )(q, k, v)


def _dil_norm_kernel(q_ref, k_ref, v_ref, gq_ref, gk_ref, *refs):
    outs, (sq_ref, sk_ref, sv_ref) = refs[:9], refs[9:]
    tm = q_ref.shape[0]
    nb = DIL_W // LANES

    def norm(x_ref, g_ref, s_ref, scale):
        for b in range(nb):
            x = x_ref[:, b * LANES:(b + 1) * LANES].astype(F32)
            lo = _lane(x.shape) < DIL_HD
            sq = x * x
            ss_lo = jnp.sum(jnp.where(lo, sq, 0.0), axis=-1, keepdims=True)
            ss_hi = jnp.sum(jnp.where(lo, 0.0, sq), axis=-1, keepdims=True)
            r = lax.rsqrt(jnp.where(lo, ss_lo, ss_hi) * (1.0 / DIL_HD) + EPS)
            s_ref[b] = x * r * g_ref[...] * scale

    norm(q_ref, gq_ref, sq_ref, (DIL_HD ** -0.5) * LOG2E)
    norm(k_ref, gk_ref, sk_ref, 1.0)
    for b in range(nb):
        sv_ref[b] = v_ref[:, b * LANES:(b + 1) * LANES].astype(F32)
    for pi, (_, d) in enumerate(DIL_PATTERNS):
        for s_ref, o_ref in zip((sq_ref, sk_ref, sv_ref), outs[3 * pi:3 * pi + 3]):
            for b in range(nb):
                sl = slice(b * LANES, (b + 1) * LANES)
                if d == 1:
                    o_ref[0, :, sl] = s_ref[b].astype(o_ref.dtype)
                else:
                    for r in range(d):
                        o_ref[r, :, sl] = s_ref[b, pl.ds(r, tm // d, stride=d), :].astype(o_ref.dtype)


def _dil_norm(main, gq, gk, tm=512):
    S = main.shape[0]
    out_specs, out_shape = [], []
    for _, d in DIL_PATTERNS:
        out_specs += [pl.BlockSpec((d, tm // d, DIL_W), lambda i: (0, i, 0))] * 3
        out_shape += [jax.ShapeDtypeStruct((d, S // d, DIL_W), BF16)] * 3
    return pl.pallas_call(
        _dil_norm_kernel,
        grid=(S // tm,),
        in_specs=[
            pl.BlockSpec((tm, DIL_W), lambda i: (i, 0)),
            pl.BlockSpec((tm, DIL_W), lambda i: (i, 1)),
            pl.BlockSpec((tm, DIL_W), lambda i: (i, 2)),
            pl.BlockSpec((1, LANES), lambda i: (0, 0)),
            pl.BlockSpec((1, LANES), lambda i: (0, 0)),
        ],
        out_specs=out_specs,
        out_shape=out_shape,
        scratch_shapes=[pltpu.VMEM((DIL_W // LANES, tm, LANES), F32)] * 3,
        compiler_params=_cparams(("parallel",)),
        name="dil_norm",
    )(main, main, main, gq, gk)


def _dil_band_kernel(q_ref, kp_ref, kc_ref, kn_ref, vp_ref, vc_ref, vn_ref, b_ref, o_ref, st_ref):
    ub = pl.program_id(1)
    nub = pl.num_programs(1)
    tq = q_ref.shape[0]
    win = tq + 2 * DIL_R
    col = _lane((1, win))
    prev_pen = jnp.where(ub > 0, 0.0, NEG)
    next_pen = jnp.where(ub < nub - 1, 0.0, NEG)
    edge = jnp.where(col < DIL_R, prev_pen, jnp.where(col >= DIL_R + tq, next_pen, 0.0))
    kwin = jnp.concatenate([kp_ref[...], kc_ref[...], kn_ref[...]], axis=0)
    vwin = jnp.concatenate([vp_ref[...], vc_ref[...], vn_ref[...]], axis=0)
    q = q_ref[...]
    st_lane = _lane((tq, LANES))
    st = jnp.zeros((tq, LANES), F32)
    for h in range(DIL_HEADS):
        sl = slice(h * DIL_HD, (h + 1) * DIL_HD)
        s = _dot_nt(q[:, sl], kwin[:, sl]) + b_ref[h] + edge
        m = jnp.max(s, axis=-1, keepdims=True)
        p = jnp.exp2(s - m)
        den = jnp.sum(p, axis=-1, keepdims=True)
        o = _dot(p.astype(BF16), vwin[:, sl]) / den
        o_ref[:, sl] = o.astype(o_ref.dtype)
        st = jnp.where(st_lane == h, m, jnp.where(st_lane == h + LANES // 2, den, st))
    st_ref[...] = st


def _dil_band(q, k, v, bias, tq=128):
    d, sub, _ = q.shape
    hb = tq // DIL_R
    last = sub // DIL_R - 1
    cur = pl.BlockSpec((None, tq, DIL_W), lambda r, u: (r, u, 0))
    prev = pl.BlockSpec((None, DIL_R, DIL_W), lambda r, u: (r, jnp.maximum(u * hb - 1, 0), 0))
    nxt = pl.BlockSpec((None, DIL_R, DIL_W), lambda r, u: (r, jnp.minimum(u * hb + hb, last), 0))
    return pl.pallas_call(
        _dil_band_kernel,
        grid=(d, sub // tq),
        in_specs=[cur, prev, cur, nxt, prev, cur, nxt,
                  pl.BlockSpec((DIL_HEADS, tq, tq + 2 * DIL_R), lambda r, u: (0, 0, 0))],
        out_specs=[cur, pl.BlockSpec((None, tq, LANES), lambda r, u: (r, u, 0))],
        out_shape=[jax.ShapeDtypeStruct((d, sub, DIL_W), BF16), jax.ShapeDtypeStruct((d, sub, LANES), F32)],
        compiler_params=_cparams(("parallel", "parallel")),
        name=f"dil_band{d}",
    )(q, k, k, k, v, v, v, bias)


def _log_sigmoid(x):
    return jnp.minimum(x, 0.0) - jnp.log(1.0 + jnp.exp(-jnp.abs(x)))


def _ml_prep_kernel(up_ref, uc_ref, un_ref, cw_ref, cb_ref, wqk_ref, g_ref, gb_ref,
                    q_ref, k_ref, gcol_ref, grow_ref):
    i = pl.program_id(0)
    tm = uc_ref.shape[0]
    pm = jnp.where(i > 0, 1.0, 0.0)
    nm = jnp.where(i < pl.num_programs(0) - 1, 1.0, 0.0)
    xc = jnp.concatenate([up_ref[...].astype(F32) * pm, uc_ref[...].astype(F32), un_ref[...].astype(F32) * nm],
                         axis=0)
    acc = jnp.zeros((tm, ML_WP), F32) + cb_ref[...]
    for j in range(ML_CONV):
        off = HALO - ML_CONV // 2 + j
        acc = acc + xc[off:off + tm] * cw_ref[j:j + 1, :]
    u = (acc * jax.nn.sigmoid(acc)).astype(BF16)
    qk = _dot(u, wqk_ref[...])
    q_ref[...] = qk[:, :ML_WP].astype(q_ref.dtype)
    k_ref[...] = qk[:, ML_WP:].astype(k_ref.dtype)

    g = g_ref[...] + gb_ref[...]
    lane = _lane(g.shape)
    is_f = (lane % 8) >= ML_HEADS
    gcol = jnp.where(lane < 4 * ML_HEADS, jnp.where(is_f, _log_sigmoid(g), g), 0.0)
    gcol_ref[...] = gcol
    grow_ref[...] = gcol.T


def _ml_prep(main, gates, cw, cb, wqk, gb, tm=512):
    S = main.shape[0]
    hb = tm // HALO
    last = S // HALO - 1
    full = lambda shape: pl.BlockSpec(shape, lambda i: (0, 0))
    return pl.pallas_call(
        _ml_prep_kernel,
        grid=(S // tm,),
        in_specs=[
            pl.BlockSpec((HALO, ML_WP), lambda i: (jnp.maximum(i * hb - 1, 0), 3)),
            pl.BlockSpec((tm, ML_WP), lambda i: (i, 3)),
            pl.BlockSpec((HALO, ML_WP), lambda i: (jnp.minimum(i * hb + hb, last), 3)),
            full((HALO, ML_WP)), full((1, ML_WP)), full((ML_WP, 2 * ML_WP)),
            pl.BlockSpec((tm, LANES), lambda i: (i, 0)),
            full((1, LANES)),
        ],
        out_specs=[
            pl.BlockSpec((tm, ML_WP), lambda i: (i, 0)),
            pl.BlockSpec((tm, ML_WP), lambda i: (i, 0)),
            pl.BlockSpec((tm, LANES), lambda i: (i, 0)),
            pl.BlockSpec((LANES, tm), lambda i: (0, i)),
        ],
        out_shape=[
            jax.ShapeDtypeStruct((S, ML_WP), BF16), jax.ShapeDtypeStruct((S, ML_WP), BF16),
            jax.ShapeDtypeStruct((S, LANES), F32), jax.ShapeDtypeStruct((LANES, S), F32),
        ],
        compiler_params=_cparams(("parallel",)),
        name="ml_prep",
    )(main, main, main, cw, cb, wqk, gates, gb)


def _ml_scan_kernel(qf_ref, kf_ref, vf_ref, gcf_ref, grf_ref, qb_ref, kb_ref, vb_ref, gcb_ref, grb_ref,
                    hf_ref, hb_ref, c_ref, n_ref, m_ref):
    L = ML_CHUNK

    @pl.when(pl.program_id(0) == 0)
    def _():
        c_ref[...] = jnp.zeros_like(c_ref)
        n_ref[...] = jnp.zeros_like(n_ref)
        m_ref[...] = jnp.zeros_like(m_ref)

    row = lax.broadcasted_iota(jnp.int32, (L, L), 0)
    colm = lax.broadcasted_iota(jnp.int32, (L, L), 1)
    lower = colm <= row
    upper = colm >= row
    t_lower = jnp.where(lower, 1.0, 0.0).astype(BF16)
    t_upper = jnp.where(upper, 1.0, 0.0).astype(BF16)

    def tri_left(t, x):
        hi, mid, lo = _split3(x)
        return _dot(t, hi) + _dot(t, mid) + _dot(t, lo)

    def tri_right(x, t):
        hi, mid, lo = _split3(x)
        return _dot(hi, t) + _dot(mid, t) + _dot(lo, t)

    dirs = (
        (qf_ref, kf_ref, vf_ref, gcf_ref, grf_ref, hf_ref, t_lower, t_upper, lower, L - 1),
        (qb_ref, kb_ref, vb_ref, gcb_ref, grb_ref, hb_ref, t_upper, t_lower, upper, 0),
    )
    for dd, (q_ref, k_ref, v_ref, gc_ref, gr_ref, h_ref, t_col, t_row, mask, last) in enumerate(dirs):
        gcol = gc_ref[...]
        grow = gr_ref[...]
        bcol = tri_left(t_col, gcol)
        brow = tri_right(grow, t_row)
        for h in range(ML_HEADS):
            ch = dd * ML_HEADS + h
            li = dd * 2 * ML_HEADS + h
            lf = li + ML_HEADS
            sl = slice(h * ML_HP, (h + 1) * ML_HP)
            q = q_ref[:, sl]
            k = k_ref[:, sl]
            v = v_ref[:, sl]
            b_c = bcol[:, lf:lf + 1]
            b_r = brow[lf:lf + 1, :]
            li_c = gcol[:, li:li + 1]
            li_r = grow[li:li + 1, :]
            m_prev = m_ref[ch:ch + 1, 0:1]
            c_prev = c_ref[ch]
            n_prev = n_ref[ch]

            logw = jnp.where(mask, b_c - b_r + li_r, NEG)
            inter = b_c + m_prev
            m = jnp.maximum(jnp.max(logw, axis=-1, keepdims=True), inter)
            a = jnp.exp(logw - m) * _dot_nt(q, k)
            decay = jnp.exp(inter - m)
            num = _dot(a.astype(BF16), v) + decay * _dot(q, c_prev.astype(BF16))
            qn = jnp.sum(q.astype(F32) * n_prev, axis=-1, keepdims=True)
            den = jnp.sum(a, axis=-1, keepdims=True) + decay * qn
            h_ref[:, sl] = num / jnp.maximum(jnp.abs(den), jnp.exp(-m))

            m_new = m[last:last + 1, :]
            b_last = b_c[last:last + 1, :]
            g = jnp.exp(b_last - b_c + li_c - m_new)
            cd = jnp.exp(b_last + m_prev - m_new)
            kg = k.astype(F32) * g
            c_ref[ch] = cd * c_prev + _dot_tn(kg.astype(BF16), v)
            n_ref[ch] = cd * n_prev + jnp.sum(kg, axis=0, keepdims=True)
            m_ref[ch:ch + 1, :] = jnp.broadcast_to(m_new, (1, LANES))


def _ml_scan(q, k, main, gcol, grow):
    S = q.shape[0]
    L = ML_CHUNK
    nc = S // L
    fwd = lambda j: (j, 0)
    bwd = lambda j: (nc - 1 - j, 0)
    blk = lambda f: pl.BlockSpec((L, ML_WP), f)
    vblk = lambda f: pl.BlockSpec((L, ML_WP), lambda j: (f(j)[0], 4))
    out = jax.ShapeDtypeStruct((S, ML_WP), F32)
    return pl.pallas_call(
        _ml_scan_kernel,
        grid=(nc,),
        in_specs=[
            blk(fwd), blk(fwd), vblk(fwd), pl.BlockSpec((L, LANES), fwd), pl.BlockSpec((LANES, L), lambda j: (0, j)),
            blk(bwd), blk(bwd), vblk(bwd), pl.BlockSpec((L, LANES), bwd),
            pl.BlockSpec((LANES, L), lambda j: (0, nc - 1 - j)),
        ],
        out_specs=[blk(fwd), blk(bwd)],
        out_shape=[out, out],
        scratch_shapes=[
            pltpu.VMEM((2 * ML_HEADS, ML_HP, ML_HP), F32),
            pltpu.VMEM((2 * ML_HEADS, 1, ML_HP), F32),
            pltpu.VMEM((2 * ML_HEADS, LANES), F32),
        ],
        compiler_params=_cparams(("arbitrary",)),
        name="ml_scan",
    )(q, k, main, gcol, grow, q, k, main, gcol, grow)


def _outproj_kernel(x_ref, ymla_ref, o1_ref, o2_ref, o3_ref, s1_ref, s2_ref, s3_ref, hf_ref, hb_ref, mo_ref,
                    gout_ref, e_ref, w1_ref, w2_ref, w3_ref, out_ref, so_ref, sst_ref):
    tm = x_ref.shape[0]

    def token_order(src_ref, scr_ref):
        d, _, width = src_ref.shape
        if d == 1:
            return src_ref[0].astype(F32)
        for b in range(width // LANES):
            for r in range(d):
                scr_ref[b, pl.ds(r, tm // d, stride=d), :] = src_ref[r, :, b * LANES:(b + 1) * LANES].astype(F32)
        return jnp.concatenate([scr_ref[b] for b in range(width // LANES)], axis=1)

    sts = [token_order(s_ref, sst_ref.at[i]) for i, s_ref in enumerate((s1_ref, s2_ref, s3_ref))]
    m_all = jnp.maximum(jnp.maximum(sts[0], sts[1]), sts[2])
    ws = [jnp.exp2(st - m_all) * pltpu.roll(st, LANES // 2, 1) for st in sts]
    inv = 1.0 / (ws[0] + ws[1] + ws[2])
    lane = _lane(inv.shape)
    ydil = jnp.zeros((tm, DIL_W), F32)
    for i, (w, o_ref) in enumerate(zip(ws, (o1_ref, o2_ref, o3_ref))):
        alpha = jnp.where(lane < DIL_HEADS, w * inv, 0.0)
        hi, mid, lo = _split3(alpha)
        e = e_ref[...]
        ydil = ydil + (_dot(hi, e) + _dot(mid, e) + _dot(lo, e)) * token_order(o_ref, so_ref.at[i])

    hs = hf_ref[...] + hb_ref[...]
    yml = []
    for h in range(ML_HEADS):
        sl = slice(h * ML_HP, (h + 1) * ML_HP)
        hh = hs[:, sl]
        ss = jnp.sum(hh * hh, axis=-1, keepdims=True) * (1.0 / ML_HD)
        yml.append(hh * lax.rsqrt(ss + EPS) * gout_ref[:, sl] * jax.nn.sigmoid(mo_ref[:, sl].astype(F32)))
    yml = jnp.concatenate(yml, axis=-1)

    out_ref[...] = (x_ref[...] + _dot(ymla_ref[...], w1_ref[...]) + _dot(ydil.astype(BF16), w2_ref[...])
                    + _dot(yml.astype(BF16), w3_ref[...]))


def _outproj(x, ymla, dil_o, dil_st, hf, hb, main, gout, e, w1, w2, w3, tm=512):
    S, D = x.shape
    row = lambda w: pl.BlockSpec((tm, w), lambda i: (i, 0))
    full = lambda shape: pl.BlockSpec(shape, lambda i: (0, 0))
    res = lambda w: [pl.BlockSpec((d, tm // d, w), lambda i: (0, i, 0)) for _, d in DIL_PATTERNS]
    return pl.pallas_call(
        _outproj_kernel,
        grid=(S // tm,),
        in_specs=[
            row(D), row(MLA_HEADS * MLA_V), *res(DIL_W), *res(LANES),
            row(ML_WP), row(ML_WP), pl.BlockSpec((tm, ML_WP), lambda i: (i, 5)),
            full((1, ML_WP)), full((LANES, DIL_W)), full((MLA_HEADS * MLA_V, D)), full((DIL_W, D)), full((ML_WP, D)),
        ],
        out_specs=row(D),
        out_shape=jax.ShapeDtypeStruct((S, D), F32),
        scratch_shapes=[pltpu.VMEM((len(DIL_PATTERNS), DIL_W // LANES, tm, LANES), F32),
                        pltpu.VMEM((len(DIL_PATTERNS), 1, tm, LANES), F32)],
        compiler_params=_cparams(("parallel",)),
        name="outproj",
    )(x, ymla, *dil_o, *dil_st, hf, hb, main, gout, e, w1, w2, w3)


def _router_kernel(x_ref, g_ref, wr_ref, br_ref, xn_ref, route_ref):
    x = x_ref[...]
    xn = x * lax.rsqrt(jnp.mean(x * x, axis=-1, keepdims=True) + EPS) * g_ref[...]
    xn_ref[...] = _pack_pairs(xn)
    x1, x2, x3 = _split3(xn)
    w1 = wr_ref[0]
    w2 = wr_ref[1]
    w3 = wr_ref[2]
    logits = (_dot(x1, w1) + (_dot(x1, w2) + _dot(x2, w1)) + (_dot(x1, w3) + _dot(x2, w2) + _dot(x3, w1))
              + br_ref[...])
    lane = _lane(logits.shape)
    big = jnp.int32(4 * LANES)

    def first_max(mask):
        v = jnp.max(jnp.where(mask, logits, NEG), axis=-1, keepdims=True)
        idx = jnp.min(jnp.where(mask & (logits == v), lane, big), axis=-1, keepdims=True)
        return v, idx

    gmask = lane < N_GROUPS
    gmax, grp = first_max(gmask)
    p_grp = 1.0 / jnp.sum(jnp.where(gmask, jnp.exp(logits - gmax), 0.0), axis=-1, keepdims=True)
    e_lo = N_GROUPS + grp * EXPERTS_PER_GROUP
    emask = (lane >= e_lo) & (lane < e_lo + EXPERTS_PER_GROUP)
    v1, i1 = first_max(emask)
    v2, i2 = first_max(emask & (lane != i1))
    t = jnp.exp(v2 - v1)
    w_1 = p_grp / (1.0 + t)
    w_2 = p_grp * t / (1.0 + t)
    e_1 = (i1 - N_GROUPS).astype(F32)
    e_2 = (i2 - N_GROUPS).astype(F32)
    route_ref[...] = jnp.where(lane == 0, e_1, jnp.where(lane == 1, e_2,
                               jnp.where(lane == 2, w_1, jnp.where(lane == 3, w_2, 0.0))))


def _router(x, g, wr, br, tm=512):
    S, D = x.shape
    return pl.pallas_call(
        _router_kernel,
        grid=(S // tm,),
        in_specs=[
            pl.BlockSpec((tm, D), lambda i: (i, 0)),
            pl.BlockSpec((1, D), lambda i: (0, 0)),
            pl.BlockSpec((3, D, LANES), lambda i: (0, 0, 0)),
            pl.BlockSpec((1, LANES), lambda i: (0, 0)),
        ],
        out_specs=[pl.BlockSpec((tm, D // 2), lambda i: (i, 0)), pl.BlockSpec((tm, LANES), lambda i: (i, 0))],
        out_shape=[jax.ShapeDtypeStruct((S, D // 2), jnp.uint32), jax.ShapeDtypeStruct((S, LANES), F32)],
        compiler_params=_cparams(("parallel",)),
        name="router",
    )(x, g, wr, br)


def _pack_pairs(x):
    n = x.shape[1] // 2
    lo = lax.bitcast_convert_type(x[:, :n].astype(BF16).astype(F32), jnp.uint32)
    hi = lax.bitcast_convert_type(x[:, n:].astype(BF16).astype(F32), jnp.uint32)
    return (hi & jnp.uint32(0xFFFF0000)) | (lo >> 16)


def _unpack_pairs(w):
    lo = lax.bitcast_convert_type(w << 16, F32)
    hi = lax.bitcast_convert_type(w & jnp.uint32(0xFFFF0000), F32)
    return jnp.concatenate([lo, hi], axis=1)


def _route_plan(route, tile, n_tiles):
    S = route.shape[0]
    P = 2 * S
    hp = lax.Precision.HIGHEST
    e = route[:, :2].astype(jnp.int32).reshape(P)
    oh = (e[:, None] == jnp.arange(N_EXPERTS, dtype=jnp.int32)[None, :]).astype(F32)
    nb = P // LANES
    ohb = oh.reshape(nb, LANES, N_EXPERTS)
    within = jnp.einsum("ts,bse->bte", jnp.tril(jnp.ones((LANES, LANES), F32)), ohb, precision=hp)
    tot = within[:, -1, :]
    before = jnp.einsum("cb,be->ce", jnp.tril(jnp.ones((nb, nb), F32), -1), tot, precision=hp)
    rank = jnp.sum((within + before[:, None, :]) * ohb, axis=-1).reshape(P) - 1.0
    counts = jnp.sum(tot, axis=0).astype(jnp.int32)
    tiles_e = (counts + tile - 1) // tile
    tile_end = jnp.cumsum(tiles_e)
    row_start = ((tile_end - tiles_e) * tile).astype(F32)
    dest = (jnp.sum(oh * row_start[None, :], axis=1) + rank).astype(jnp.int32)
    n_active = tile_end[-1]
    tidx = jnp.minimum(jnp.arange(n_tiles, dtype=jnp.int32), n_active - 1)
    tile_expert = jnp.sum((tidx[:, None] >= tile_end[None, :]).astype(jnp.int32), axis=1)
    return dest, tile_expert.astype(jnp.int32), n_active.reshape(1).astype(jnp.int32)


def _row_copies(n, make):
    def start(t, c):
        make(t, 0).start()
        make(t, 1).start()
        return c

    def wait(t, c):
        make(t, 0).wait()
        make(t, 1).wait()
        return c

    lax.fori_loop(0, n, start, 0)
    lax.fori_loop(0, n, wait, 0)


def _dispatch_kernel(dest_ref, xn_ref, xs_in_ref, xs_ref, sem):
    del xs_in_ref
    tm = xn_ref.shape[0]
    _row_copies(tm, lambda t, slot: pltpu.make_async_copy(
        xn_ref.at[pl.ds(t, 1)], xs_ref.at[pl.ds(dest_ref[0, 2 * t + slot], 1)], sem))


def _dispatch(xn, dest, n_rows, tm=512):
    S, W = xn.shape
    return pl.pallas_call(
        _dispatch_kernel,
        grid=(S // tm,),
        in_specs=[
            pl.BlockSpec((None, 1, 2 * tm), lambda i: (i, 0, 0), memory_space=pltpu.SMEM),
            pl.BlockSpec((tm, W), lambda i: (i, 0)),
            pl.BlockSpec(memory_space=pl.ANY),
        ],
        out_specs=pl.BlockSpec(memory_space=pl.ANY),
        out_shape=jax.ShapeDtypeStruct((n_rows, W), xn.dtype),
        scratch_shapes=[pltpu.SemaphoreType.DMA(())],
        input_output_aliases={2: 0},
        compiler_params=_cparams(("arbitrary",)),
        name="moe_dispatch",
    )(dest.reshape(S // tm, 1, 2 * tm), xn, jnp.zeros((n_rows, W), xn.dtype))


def _experts_kernel(te_ref, na_ref, xs_ref, wg_ref, wu_ref, wd_ref, y_ref, wgb_ref, wub_ref, wdb_ref):
    i = pl.program_id(0)

    @pl.when(i < na_ref[0])
    def _():
        @pl.when((i == 0) | (te_ref[i] != te_ref[jnp.maximum(i - 1, 0)]))
        def _():
            wgb_ref[...] = wg_ref[...].astype(BF16)
            wub_ref[...] = wu_ref[...].astype(BF16)
            wdb_ref[...] = wd_ref[...].astype(BF16)

        x = _unpack_pairs(xs_ref[...]).astype(BF16)
        gate = _dot(x, wgb_ref[...])
        up = _dot(x, wub_ref[...])
        hid = (gate * jax.nn.sigmoid(gate) * up).astype(BF16)
        y_ref[...] = _pack_pairs(_dot(hid, wdb_ref[...]))

    @pl.when(i >= na_ref[0])
    def _():
        y_ref[...] = jnp.zeros_like(y_ref)


def _experts(xs, tile_expert, n_active, wg, wu, wd, layer, tile):
    R, W = xs.shape
    D = 2 * W
    row = lambda i, te, na: (jnp.minimum(i, na[0] - 1), 0)
    wsel = lambda i, te, na: (layer, te[i], 0, 0)
    return pl.pallas_call(
        _experts_kernel,
        grid_spec=pltpu.PrefetchScalarGridSpec(
            num_scalar_prefetch=2,
            grid=(R // tile,),
            in_specs=[
                pl.BlockSpec((tile, W), row),
                pl.BlockSpec((None, None, D, D_EXPERT), wsel),
                pl.BlockSpec((None, None, D, D_EXPERT), wsel),
                pl.BlockSpec((None, None, D_EXPERT, D), wsel),
            ],
            out_specs=pl.BlockSpec((tile, W), lambda i, te, na: (i, 0)),
            scratch_shapes=[pltpu.VMEM((D, D_EXPERT), BF16), pltpu.VMEM((D, D_EXPERT), BF16),
                            pltpu.VMEM((D_EXPERT, D), BF16)],
        ),
        out_shape=jax.ShapeDtypeStruct((R, W), jnp.uint32),
        compiler_params=_cparams(("arbitrary",)),
        name="moe_experts",
    )(tile_expert, n_active, xs, wg, wu, wd)


def _combine_kernel(dest_ref, x_ref, route_ref, y_ref, out_ref, ya_ref, yb_ref, sem):
    tm = x_ref.shape[0]
    bufs = (ya_ref, yb_ref)
    _row_copies(tm, lambda t, slot: pltpu.make_async_copy(
        y_ref.at[pl.ds(dest_ref[0, 2 * t + slot], 1)], bufs[slot].at[pl.ds(t, 1)], sem))
    route = route_ref[...]
    lane = _lane(route.shape)
    w_1 = jnp.sum(jnp.where(lane == 2, route, 0.0), axis=-1, keepdims=True)
    w_2 = jnp.sum(jnp.where(lane == 3, route, 0.0), axis=-1, keepdims=True)
    out_ref[...] = x_ref[...] + w_1 * _unpack_pairs(ya_ref[...]) + w_2 * _unpack_pairs(yb_ref[...])


def _combine(x, route, y, dest, tm=256):
    S, D = x.shape
    return pl.pallas_call(
        _combine_kernel,
        grid=(S // tm,),
        in_specs=[
            pl.BlockSpec((None, 1, 2 * tm), lambda i: (i, 0, 0), memory_space=pltpu.SMEM),
            pl.BlockSpec((tm, D), lambda i: (i, 0)),
            pl.BlockSpec((tm, LANES), lambda i: (i, 0)),
            pl.BlockSpec(memory_space=pl.ANY),
        ],
        out_specs=pl.BlockSpec((tm, D), lambda i: (i, 0)),
        out_shape=jax.ShapeDtypeStruct((S, D), F32),
        scratch_shapes=[pltpu.VMEM((tm, D // 2), jnp.uint32), pltpu.VMEM((tm, D // 2), jnp.uint32),
                        pltpu.SemaphoreType.DMA(())],
        compiler_params=_cparams(("arbitrary",)),
        name="moe_combine",
    )(dest.reshape(S // tm, 1, 2 * tm), x, route, y)


def _moe(x, xn, route, wg, wu, wd, layer, tile=256):
    S = x.shape[0]
    n_rows = 2 * S + N_EXPERTS * tile
    dest, tile_expert, n_active = _route_plan(route, tile, n_rows // tile)
    xs = _dispatch(xn, dest, n_rows)
    y = _experts(xs, tile_expert, n_active, wg, wu, wd, layer, tile)
    return _combine(x, route, y, dest)


def _pad_cols(w, width):
    return jnp.pad(w, ((0, 0), (0, width - w.shape[1])))


def _pad_heads(w):
    lead = w.shape[:-1]
    w = w.reshape(*lead, ML_HEADS, ML_HD)
    w = jnp.pad(w, [(0, 0)] * len(lead) + [(0, 0), (0, ML_HP - ML_HD)])
    return w.reshape(*lead, ML_WP)


def _t5_buckets(rel):
    half = N_BUCKETS // 2
    max_exact = half // 2
    n = np.abs(rel)
    large = max_exact + (np.log(np.maximum(n, 1) / max_exact) / np.log(BUCKET_MAX_DIST / max_exact)
                         * (half - max_exact)).astype(np.int32)
    large = np.minimum(large, half - 1)
    return (rel > 0).astype(np.int32) * half + np.where(n < max_exact, n, large).astype(np.int32)


def _band_bias(rel_bias, d, tq=128):
    win = tq + 2 * DIL_R
    offs = (np.arange(2 * DIL_R + 1) - DIL_R) * d
    t = rel_bias.astype(F32)[jnp.asarray(_t5_buckets(offs))] * LOG2E
    v = jnp.concatenate([t, jnp.full((win + 1 - t.shape[0], DIL_HEADS), NEG, F32)], axis=0)
    b = jnp.tile(v, (tq, 1))[:tq * win].reshape(tq, win, DIL_HEADS)
    return jnp.transpose(b, (2, 0, 1))


def _layer_weights(l, p):
    w_in = p["w_in"][l]
    off = np.cumsum((0, MLA_Q_RANK, MLA_KV_RANK, MLA_ROPE, DIL_W, DIL_W, DIL_W, ML_W, ML_W, ML_W, 4 * ML_HEADS))
    col = lambda a, b: w_in[:, off[a]:off[b]]
    w_main = jnp.concatenate([
        col(3, 6),
        _pad_cols(col(0, 3), DIL_W),
        _pad_cols(col(6, 7), ML_WP),
        _pad_heads(col(7, 8)), _pad_heads(col(8, 9)),
    ], axis=1).astype(BF16)
    w_gate = _pad_cols(col(9, 10), LANES).astype(BF16)

    wq = p["mla_w_uq"][l].reshape(MLA_Q_RANK, MLA_HEADS, MLA_QK)
    wq = jnp.pad(wq, ((0, 0), (0, 0), (0, LANES - MLA_QK))).reshape(MLA_Q_RANK, MLA_HEADS * LANES).astype(BF16)
    wkv = p["mla_w_ukv"][l].reshape(MLA_KV_RANK, MLA_HEADS, MLA_NOPE + MLA_V)
    padh = lambda w: jnp.pad(w, ((0, 0), (0, 0), (0, LANES - w.shape[-1]))).reshape(
        MLA_KV_RANK, MLA_HEADS * LANES).astype(BF16)
    wk = padh(wkv[..., :MLA_NOPE])
    wv = padh(wkv[..., MLA_NOPE:])
    vone = jnp.tile((jnp.arange(LANES) == MLA_V).astype(F32), MLA_HEADS)[None, :]
    lane_pad = lambda g: jnp.pad(g, (0, LANES - g.shape[0]))[None, :]

    def block_diag(w):
        out = jnp.zeros((ML_WP, ML_WP), F32)
        for h in range(ML_HEADS):
            out = out.at[h * ML_HD:(h + 1) * ML_HD, h * ML_HP:h * ML_HP + ML_HD].set(w[h])
        return out
    wqk = jnp.concatenate([block_diag(p["ml_w_q"][l]), block_diag(p["ml_w_k"][l]) * (ML_HD ** -0.5)],
                          axis=1).astype(BF16)
    cw = jnp.pad(p["ml_conv_w"][l], ((0, HALO - ML_CONV), (0, ML_WP - ML_W)))
    cb = _pad_cols(p["ml_conv_b"][l][None, :], ML_WP)
    gb = _pad_cols(p["ml_gate_bias"][l].reshape(1, 4 * ML_HEADS), LANES)

    w_out = p["w_out"][l]
    n_mla = MLA_HEADS * MLA_V
    w3 = w_out[n_mla + DIL_W:].reshape(ML_HEADS, ML_HD, D_MODEL)
    w3 = jnp.pad(w3, ((0, 0), (0, ML_HP - ML_HD), (0, 0))).reshape(ML_WP, D_MODEL)

    wr = jnp.concatenate([p["router_group_w"][l], p["router_expert_w"][l]], axis=1)
    wr = jnp.stack(_split3(_pad_cols(wr, LANES)))
    br = _pad_cols(jnp.concatenate([p["router_group_b"][l], p["router_expert_b"][l]])[None, :], LANES)
    return dict(
        norm_mix=p["norm_mix"][l][None, :], w_main=w_main, w_gate=w_gate,
        gcq=p["mla_norm_cq"][l][None, :], wq=wq, gckv=p["mla_norm_ckv"][l][None, :], wk=wk, wv=wv, vone=vone,
        gq=lane_pad(p["mla_q_norm"][l]), gk=lane_pad(p["mla_k_norm"][l]),
        dgq=jnp.tile(p["dil_q_norm"][l], 2)[None, :], dgk=jnp.tile(p["dil_k_norm"][l], 2)[None, :],
        cw=cw, cb=cb, wqk=wqk, gb=gb, gout=_pad_heads(p["ml_out_norm"][l][None, :]),
        w1=w_out[:n_mla].astype(BF16), w2=w_out[n_mla:n_mla + DIL_W].astype(BF16), w3=w3.astype(BF16),
        norm_ffn=p["norm_ffn"][l][None, :], wr=wr, br=br,
    )


def kernel(x, positions, rel_bias, norm_mix, w_in, mla_norm_cq, mla_w_uq, mla_norm_ckv, mla_w_ukv, mla_q_norm, mla_k_norm, dil_q_norm, dil_k_norm, ml_conv_w, ml_conv_b, ml_w_q, ml_w_k, ml_gate_bias, ml_out_norm, w_out, norm_ffn, router_group_w, router_group_b, router_expert_w, router_expert_b, moe_w_gate, moe_w_up, moe_w_down):
    params = dict(norm_mix=norm_mix, w_in=w_in, mla_norm_cq=mla_norm_cq, mla_w_uq=mla_w_uq,
                  mla_norm_ckv=mla_norm_ckv, mla_w_ukv=mla_w_ukv, mla_q_norm=mla_q_norm, mla_k_norm=mla_k_norm,
                  dil_q_norm=dil_q_norm, dil_k_norm=dil_k_norm, ml_conv_w=ml_conv_w, ml_conv_b=ml_conv_b,
                  ml_w_q=ml_w_q, ml_w_k=ml_w_k, ml_gate_bias=ml_gate_bias, ml_out_norm=ml_out_norm, w_out=w_out,
                  norm_ffn=norm_ffn, router_group_w=router_group_w, router_group_b=router_group_b,
                  router_expert_w=router_expert_w, router_expert_b=router_expert_b, moe_w_gate=moe_w_gate,
                  moe_w_up=moe_w_up, moe_w_down=moe_w_down)
    B, S, D = x.shape
    assert B == 1 and D == D_MODEL
    depth = w_in.shape[0]
    xs = x.reshape(S, D)
    pos = positions.reshape(S, 1)

    half = MLA_ROPE // 2
    inv_freq = ROPE_THETA ** (-jnp.arange(half, dtype=F32) / half)
    invf = jnp.zeros((LANES,), F32).at[MLA_NOPE:MLA_NOPE + half].set(inv_freq)
    invf = invf.at[MLA_NOPE + half:MLA_QK].set(inv_freq)[None, :]
    biases = [_band_bias(rel_bias, d) for _, d in DIL_PATTERNS]
    expand = (jnp.arange(LANES)[:, None] == (jnp.arange(DIL_W) // DIL_HD)[None, :]).astype(BF16)

    for l in range(depth):
        w = _layer_weights(l, params)
        main, gates = _inproj(xs, w["norm_mix"], w["w_main"], w["w_gate"])
        q, k, v = _mla_prep(main, pos, invf, w["gcq"], w["wq"], w["gckv"], w["wk"], w["wv"], w["vone"],
                            w["gq"], w["gk"])
        ymla = _mla_attn(q, k, v)
        dqkv = _dil_norm(main, w["dgq"], w["dgk"])
        dil = [_dil_band(*dqkv[3 * i:3 * i + 3], b) for i, b in enumerate(biases)]
        mq, mk, gcol, grow = _ml_prep(main, gates, w["cw"], w["cb"], w["wqk"], w["gb"])
        hf, hb = _ml_scan(mq, mk, main, gcol, grow)
        xs = _outproj(xs, ymla, [o for o, _ in dil], [s for _, s in dil], hf, hb, main, w["gout"], expand,
                      w["w1"], w["w2"], w["w3"])
        xn, route = _router(xs, w["norm_ffn"], w["wr"], w["br"])
        xs = _moe(xs, xn, route, moe_w_gate, moe_w_up, moe_w_down, l)
    return xs.reshape(B, S, D)
```

```python
import functools
import math

import numpy as np
import jax
import jax.numpy as jnp
from jax import lax
from jax.experimental import pallas as pl
from jax.experimental.pallas import tpu as pltpu

F32 = jnp.float32
BF16 = jnp.bfloat16

D_MODEL = 2048
EPS = 1e-6
NEG = -1e30
LOG2E = 1.4426950408889634

MLA_HEADS = 8
MLA_Q_RANK = 384
MLA_KV_RANK = 128
MLA_NOPE = 64
MLA_ROPE = 32
MLA_V = 64
MLA_QK = MLA_NOPE + MLA_ROPE
ROPE_THETA = 10000.0

DIL_HEADS = 12
DIL_HD = 64
DIL_W = DIL_HEADS * DIL_HD
DIL_PATTERNS = ((128, 1), (512, 4), (2048, 16))
DIL_R = 64
N_BUCKETS = 32
BUCKET_MAX_DIST = 1024

ML_HEADS = 4
ML_HD = 192
ML_HP = 256
ML_W = ML_HEADS * ML_HD
ML_WP = ML_HEADS * ML_HP
ML_CONV = 5
ML_CHUNK = 128

N_GROUPS = 4
EXPERTS_PER_GROUP = 4
N_EXPERTS = 16
D_EXPERT = 512

LANES = 128
HALO = 8
MAIN_W = 6144
VMEM_LIMIT = 56 * 1024 * 1024


def _cparams(sem):
    return pltpu.CompilerParams(dimension_semantics=sem, vmem_limit_bytes=VMEM_LIMIT)


def _lane(shape):
    return lax.broadcasted_iota(jnp.int32, shape, len(shape) - 1)


def _split3(x):
    hi = x.astype(BF16)
    r1 = x - hi.astype(F32)
    mid = r1.astype(BF16)
    lo = (r1 - mid.astype(F32)).astype(BF16)
    return hi, mid, lo


def _dot(a, b):
    return jnp.dot(a, b, preferred_element_type=F32)


def _dot_nt(a, b):
    return lax.dot_general(a, b, (((1,), (1,)), ((), ())), preferred_element_type=F32)


def _dot_tn(a, b):
    return lax.dot_general(a, b, (((0,), (0,)), ((), ())), preferred_element_type=F32)


def _inproj_kernel(x_ref, g_ref, w_ref, wg_ref, o_ref, og_ref, xn_ref):
    @pl.when(pl.program_id(1) == 0)
    def _():
        x = x_ref[...]
        ms = jnp.mean(x * x, axis=-1, keepdims=True)
        xn = (x * lax.rsqrt(ms + EPS) * g_ref[...]).astype(BF16)
        xn_ref[...] = xn
        og_ref[...] = _dot(xn, wg_ref[...])

    o_ref[...] = _dot(xn_ref[...], w_ref[...]).astype(o_ref.dtype)


def _inproj(x, g, w_main, w_gate, tm=1024, tn=768):
    S, D = x.shape
    N = w_main.shape[1]
    return pl.pallas_call(
        _inproj_kernel,
        grid=(S // tm, N // tn),
        in_specs=[
            pl.BlockSpec((tm, D), lambda i, j: (i, 0)),
            pl.BlockSpec((1, D), lambda i, j: (0, 0)),
            pl.BlockSpec((D, tn), lambda i, j: (0, j)),
            pl.BlockSpec((D, LANES), lambda i, j: (0, 0)),
        ],
        out_specs=[
            pl.BlockSpec((tm, tn), lambda i, j: (i, j)),
            pl.BlockSpec((tm, LANES), lambda i, j: (i, 0)),
        ],
        out_shape=[jax.ShapeDtypeStruct((S, N), BF16), jax.ShapeDtypeStruct((S, LANES), F32)],
        scratch_shapes=[pltpu.VMEM((tm, D), BF16)],
        compiler_params=_cparams(("parallel", "arbitrary")),
        name="inproj",
    )(x, g, w_main, w_gate)


def _rope_tables(pos_ref, invf_ref):
    ang = pos_ref[...].astype(F32) * invf_ref[...]
    lane = _lane(ang.shape)
    cos = jnp.cos(ang)
    sin = jnp.sin(ang)
    rope_a = (lane >= MLA_NOPE) & (lane < MLA_NOPE + MLA_ROPE // 2)
    rope_b = (lane >= MLA_NOPE + MLA_ROPE // 2) & (lane < MLA_QK)
    c = jnp.where(lane < MLA_NOPE, 1.0, jnp.where(lane < MLA_QK, cos, 0.0))
    s1 = jnp.where(rope_a, -sin, 0.0)
    s2 = jnp.where(rope_b, sin, 0.0)
    return c, s1, s2


def _rope(x, tabs):
    c, s1, s2 = tabs
    half = MLA_ROPE // 2
    return x * c + pltpu.roll(x, LANES - half, 1) * s1 + pltpu.roll(x, half, 1) * s2


def _mla_prep_kernel(in_ref, pos_ref, invf_ref, gcq_ref, wq_ref, gckv_ref, wk_ref, wv_ref, vone_ref,
                     gq_ref, gk_ref, q_ref, k_ref, vt_ref):
    xin = in_ref[...].astype(F32)
    cq = xin[:, :MLA_Q_RANK]
    ckv = xin[:, MLA_Q_RANK:MLA_Q_RANK + MLA_KV_RANK]
    kr_blk = xin[:, MLA_Q_RANK + MLA_KV_RANK:MLA_Q_RANK + MLA_KV_RANK + LANES]

    cqn = (cq * lax.rsqrt(jnp.mean(cq * cq, axis=-1, keepdims=True) + EPS) * gcq_ref[...]).astype(BF16)
    ckvn = (ckv * lax.rsqrt(jnp.mean(ckv * ckv, axis=-1, keepdims=True) + EPS) * gckv_ref[...]).astype(BF16)
    q = _dot(cqn, wq_ref[...])
    kn = _dot(ckvn, wk_ref[...])
    v = _dot(ckvn, wv_ref[...]) + vone_ref[...]
    vt_ref[0] = v.T.astype(vt_ref.dtype)

    tabs = _rope_tables(pos_ref, invf_ref)
    lane = _lane(kr_blk.shape)
    kr = jnp.where((lane >= MLA_NOPE) & (lane < MLA_QK), pltpu.roll(kr_blk, MLA_NOPE, 1), 0.0)
    kr = _rope(kr, tabs)

    q_scale = (MLA_QK ** -0.5) * LOG2E
    for h in range(MLA_HEADS):
        sl = slice(h * LANES, (h + 1) * LANES)
        qh = _rope(q[:, sl], tabs)
        ss = jnp.sum(qh * qh, axis=-1, keepdims=True) * (1.0 / MLA_QK)
        q_ref[:, sl] = (qh * lax.rsqrt(ss + EPS) * gq_ref[...] * q_scale).astype(q_ref.dtype)
        kh = kn[:, sl] + kr
        ss = jnp.sum(kh * kh, axis=-1, keepdims=True) * (1.0 / MLA_QK)
        k_ref[:, sl] = (kh * lax.rsqrt(ss + EPS) * gk_ref[...]).astype(k_ref.dtype)


def _mla_prep(main, pos, invf, gcq, wq, gckv, wk, wv, vone, gq, gk, tm=512):
    S = main.shape[0]
    HW = MLA_HEADS * LANES
    full = lambda shape: pl.BlockSpec(shape, lambda i: (0, 0))
    out = jax.ShapeDtypeStruct((S, HW), BF16)
    return pl.pallas_call(
        _mla_prep_kernel,
        grid=(S // tm,),
        in_specs=[
            pl.BlockSpec((tm, 768), lambda i: (i, 3)),
            pl.BlockSpec((tm, 1), lambda i: (i, 0)),
            full((1, LANES)), full((1, MLA_Q_RANK)), full((MLA_Q_RANK, HW)), full((1, MLA_KV_RANK)),
            full((MLA_KV_RANK, HW)), full((MLA_KV_RANK, HW)), full((1, HW)), full((1, LANES)), full((1, LANES)),
        ],
        out_specs=[pl.BlockSpec((tm, HW), lambda i: (i, 0))] * 2 + [pl.BlockSpec((1, HW, tm), lambda i: (i, 0, 0))],
        out_shape=[out, out, jax.ShapeDtypeStruct((S // tm, HW, tm), BF16)],
        compiler_params=_cparams(("parallel",)),
        name="mla_prep",
    )(main, pos, invf, gcq, wq, gckv, wk, wv, vone, gq, gk)


MLA_VROWS = 80


def _mla_attn_kernel(q_ref, k_ref, vt_ref, o_ref, sa_ref, sb_ref, *, tk, sub, unroll):
    nk = k_ref.shape[0] // tk
    units = [(i, hh) for i in range(q_ref.shape[0] // sub) for hh in range(2)]
    fold = 8
    s_refs = (sa_ref, sb_ref)

    def logits_chunk(u, c, mx):
        i, hh = units[u]
        sl = slice(hh * LANES, (hh + 1) * LANES)
        rows = pl.ds(pl.multiple_of(c * tk, tk), tk)
        s = _dot_nt(k_ref[rows, sl], q_ref[i * sub:(i + 1) * sub, sl])
        s_refs[u % 2][c] = s
        return jnp.maximum(mx, jnp.max(s.reshape(fold, tk // fold, sub), axis=0))

    def weigh_chunk(u, c, m, acc):
        _, hh = units[u]
        p = jnp.exp2(s_refs[u % 2][c] - m).astype(BF16)
        return acc + _dot(vt_ref[c, hh * LANES:hh * LANES + MLA_VROWS, :], p)

    mx0 = jnp.full((tk // fold, sub), NEG, F32)
    acc0 = jnp.zeros((MLA_VROWS, sub), F32)
    mx = lax.fori_loop(0, nk, lambda c, mx: logits_chunk(0, c, mx), mx0, unroll=unroll)
    for u, (i, hh) in enumerate(units):
        m = jnp.max(mx, axis=0, keepdims=True)
        if u + 1 < len(units):
            def body(c, carry, u=u, m=m):
                acc, mx = carry
                return weigh_chunk(u, c, m, acc), logits_chunk(u + 1, c, mx)
            acc, mx = lax.fori_loop(0, nk, body, (acc0, mx0), unroll=unroll)
        else:
            acc = lax.fori_loop(0, nk, lambda c, acc, u=u, m=m: weigh_chunk(u, c, m, acc), acc0, unroll=unroll)
        o_t = acc[:MLA_V] / acc[MLA_V:MLA_V + 1]
        o = jnp.concatenate([o_t, jnp.zeros_like(o_t)], axis=0).T
        o_ref[i * sub:(i + 1) * sub, hh * MLA_V:(hh + 1) * MLA_V] = o[:, :MLA_V].astype(o_ref.dtype)


def _mla_attn(q, k, vt, tq=1024, sub=256, unroll=8):
    S = q.shape[0]
    nk, _, tk = vt.shape
    return pl.pallas_call(
        functools.partial(_mla_attn_kernel, tk=tk, sub=sub, unroll=unroll),
        grid=(MLA_HEADS // 2, S // tq),
        in_specs=[
            pl.BlockSpec((tq, 2 * LANES), lambda h, i: (i, h)),
            pl.BlockSpec((S, 2 * LANES), lambda h, i: (0, h)),
            pl.BlockSpec((nk, 2 * LANES, tk), lambda h, i: (0, h, 0)),
        ],
        out_specs=pl.BlockSpec((tq, 2 * MLA_V), lambda h, i: (i, h)),
        out_shape=jax.ShapeDtypeStruct((S, MLA_HEADS * MLA_V), BF16),
        scratch_shapes=[pltpu.VMEM((nk, tk, sub), F32)] * 2,
        compiler_params=_cparams(("parallel", "parallel")),
        name="mla_attn",
    )(q, k, vt)


def _dil_norm_kernel(q_ref, k_ref, v_ref, gq_ref, gk_ref, *refs):
    outs, (sq_ref, sk_ref, sv_ref) = refs[:9], refs[9:]
    tm = q_ref.shape[0]
    nb = DIL_W // LANES

    def norm(x_ref, g_ref, s_ref, scale):
        for b in range(nb):
            x = x_ref[:, b * LANES:(b + 1) * LANES].astype(F32)
            lo = _lane(x.shape) < DIL_HD
            sq = x * x
            ss_lo = jnp.sum(jnp.where(lo, sq, 0.0), axis=-1, keepdims=True)
            ss_hi = jnp.sum(jnp.where(lo, 0.0, sq), axis=-1, keepdims=True)
            r = lax.rsqrt(jnp.where(lo, ss_lo, ss_hi) * (1.0 / DIL_HD) + EPS)
            s_ref[b] = x * r * g_ref[...] * scale

    norm(q_ref, gq_ref, sq_ref, (DIL_HD ** -0.5) * LOG2E)
    norm(k_ref, gk_ref, sk_ref, 1.0)
    for b in range(nb):
        sv_ref[b] = v_ref[:, b * LANES:(b + 1) * LANES].astype(F32)
    for pi, (_, d) in enumerate(DIL_PATTERNS):
        for s_ref, o_ref in zip((sq_ref, sk_ref, sv_ref), outs[3 * pi:3 * pi + 3]):
            for b in range(nb):
                sl = slice(b * LANES, (b + 1) * LANES)
                if d == 1:
                    o_ref[0, :, sl] = s_ref[b].astype(o_ref.dtype)
                else:
                    for r in range(d):
                        o_ref[r, :, sl] = s_ref[b, pl.ds(r, tm // d, stride=d), :].astype(o_ref.dtype)


def _dil_norm(main, gq, gk, tm=512):
    S = main.shape[0]
    out_specs, out_shape = [], []
    for _, d in DIL_PATTERNS:
        out_specs += [pl.BlockSpec((d, tm // d, DIL_W), lambda i: (0, i, 0))] * 3
        out_shape += [jax.ShapeDtypeStruct((d, S // d, DIL_W), BF16)] * 3
    return pl.pallas_call(
        _dil_norm_kernel,
        grid=(S // tm,),
        in_specs=[
            pl.BlockSpec((tm, DIL_W), lambda i: (i, 0)),
            pl.BlockSpec((tm, DIL_W), lambda i: (i, 1)),
            pl.BlockSpec((tm, DIL_W), lambda i: (i, 2)),
            pl.BlockSpec((1, LANES), lambda i: (0, 0)),
            pl.BlockSpec((1, LANES), lambda i: (0, 0)),
        ],
        out_specs=out_specs,
        out_shape=out_shape,
        scratch_shapes=[pltpu.VMEM((DIL_W // LANES, tm, LANES), F32)] * 3,
        compiler_params=_cparams(("parallel",)),
        name="dil_norm",
    )(main, main, main, gq, gk)


def _dil_band_kernel(q_ref, kp_ref, kc_ref, kn_ref, vp_ref, vc_ref, vn_ref, b_ref, o_ref, st_ref):
    ub = pl.program_id(1)
    nub = pl.num_programs(1)
    tq = q_ref.shape[0]
    win = tq + 2 * DIL_R
    col = _lane((1, win))
    prev_pen = jnp.where(ub > 0, 0.0, NEG)
    next_pen = jnp.where(ub < nub - 1, 0.0, NEG)
    edge = jnp.where(col < DIL_R, prev_pen, jnp.where(col >= DIL_R + tq, next_pen, 0.0))
    kwin = jnp.concatenate([kp_ref[...], kc_ref[...], kn_ref[...]], axis=0)
    vwin = jnp.concatenate([vp_ref[...], vc_ref[...], vn_ref[...]], axis=0)
    st_lane = _lane((tq, LANES))
    low = st_lane < DIL_HD
    st = jnp.zeros((tq, LANES), F32)
    zero = jnp.zeros((), BF16)
    for hp in range(DIL_HEADS // 2):
        sl = slice(hp * LANES, (hp + 1) * LANES)
        q2, k2, v2 = q_ref[:, sl], kwin[:, sl], vwin[:, sl]
        outs, invs = [], []
        for half in range(2):
            h = 2 * hp + half
            qh = jnp.where(low, q2, zero) if half == 0 else jnp.where(low, zero, q2)
            s = _dot_nt(qh, k2) + b_ref[h] + edge
            m = jnp.max(s, axis=-1, keepdims=True)
            p = jnp.exp2(s - m)
            den = jnp.sum(p, axis=-1, keepdims=True)
            outs.append(_dot(p.astype(BF16), v2))
            invs.append(1.0 / den)
            st = jnp.where(st_lane == h, m, jnp.where(st_lane == h + LANES // 2, den, st))
        o_ref[:, sl] = (jnp.where(low, outs[0], outs[1]) * jnp.where(low, invs[0], invs[1])).astype(o_ref.dtype)
    st_ref[...] = st


def _dil_band(q, k, v, bias, tq=128):
    d, sub, _ = q.shape
    hb = tq // DIL_R
    last = sub // DIL_R - 1
    cur = pl.BlockSpec((None, tq, DIL_W), lambda r, u: (r, u, 0))
    prev = pl.BlockSpec((None, DIL_R, DIL_W), lambda r, u: (r, jnp.maximum(u * hb - 1, 0), 0))
    nxt = pl.BlockSpec((None, DIL_R, DIL_W), lambda r, u: (r, jnp.minimum(u * hb + hb, last), 0))
    return pl.pallas_call(
        _dil_band_kernel,
        grid=(d, sub // tq),
        in_specs=[cur, prev, cur, nxt, prev, cur, nxt,
                  pl.BlockSpec((DIL_HEADS, tq, tq + 2 * DIL_R), lambda r, u: (0, 0, 0))],
        out_specs=[cur, pl.BlockSpec((None, tq, LANES), lambda r, u: (r, u, 0))],
        out_shape=[jax.ShapeDtypeStruct((d, sub, DIL_W), BF16), jax.ShapeDtypeStruct((d, sub, LANES), F32)],
        compiler_params=_cparams(("parallel", "parallel")),
        name=f"dil_band{d}",
    )(q, k, k, k, v, v, v, bias)


def _log_sigmoid(x):
    return jnp.minimum(x, 0.0) - jnp.log(1.0 + jnp.exp(-jnp.abs(x)))


def _ml_prep_kernel(up_ref, uc_ref, un_ref, cw_ref, cb_ref, wqk_ref, g_ref, gb_ref,
                    q_ref, k_ref, gcol_ref, grow_ref):
    i = pl.program_id(0)
    tm = uc_ref.shape[0]
    pm = jnp.where(i > 0, 1.0, 0.0)
    nm = jnp.where(i < pl.num_programs(0) - 1, 1.0, 0.0)
    xc = jnp.concatenate([up_ref[...].astype(F32) * pm, uc_ref[...].astype(F32), un_ref[...].astype(F32) * nm],
                         axis=0)
    acc = jnp.zeros((tm, ML_WP), F32) + cb_ref[...]
    for j in range(ML_CONV):
        off = HALO - ML_CONV // 2 + j
        acc = acc + xc[off:off + tm] * cw_ref[j:j + 1, :]
    u = (acc * jax.nn.sigmoid(acc)).astype(BF16)
    qk = _dot(u, wqk_ref[...])
    q_ref[...] = qk[:, :ML_WP].astype(q_ref.dtype)
    k_ref[...] = qk[:, ML_WP:].astype(k_ref.dtype)

    g = g_ref[...] + gb_ref[...]
    lane = _lane(g.shape)
    is_f = (lane % 8) >= ML_HEADS
    gcol = jnp.where(lane < 4 * ML_HEADS, jnp.where(is_f, _log_sigmoid(g), g), 0.0)
    gcol_ref[...] = gcol
    grow_ref[...] = gcol.T


def _ml_prep(main, gates, cw, cb, wqk, gb, tm=512):
    S = main.shape[0]
    hb = tm // HALO
    last = S // HALO - 1
    full = lambda shape: pl.BlockSpec(shape, lambda i: (0, 0))
    return pl.pallas_call(
        _ml_prep_kernel,
        grid=(S // tm,),
        in_specs=[
            pl.BlockSpec((HALO, ML_WP), lambda i: (jnp.maximum(i * hb - 1, 0), 3)),
            pl.BlockSpec((tm, ML_WP), lambda i: (i, 3)),
            pl.BlockSpec((HALO, ML_WP), lambda i: (jnp.minimum(i * hb + hb, last), 3)),
            full((HALO, ML_WP)), full((1, ML_WP)), full((ML_WP, 2 * ML_WP)),
            pl.BlockSpec((tm, LANES), lambda i: (i, 0)),
            full((1, LANES)),
        ],
        out_specs=[
            pl.BlockSpec((tm, ML_WP), lambda i: (i, 0)),
            pl.BlockSpec((tm, ML_WP), lambda i: (i, 0)),
            pl.BlockSpec((tm, LANES), lambda i: (i, 0)),
            pl.BlockSpec((LANES, tm), lambda i: (0, i)),
        ],
        out_shape=[
            jax.ShapeDtypeStruct((S, ML_WP), BF16), jax.ShapeDtypeStruct((S, ML_WP), BF16),
            jax.ShapeDtypeStruct((S, LANES), F32), jax.ShapeDtypeStruct((LANES, S), F32),
        ],
        compiler_params=_cparams(("parallel",)),
        name="ml_prep",
    )(main, main, main, cw, cb, wqk, gates, gb)


def _ml_scan_kernel(qf_ref, kf_ref, vf_ref, gcf_ref, grf_ref, qb_ref, kb_ref, vb_ref, gcb_ref, grb_ref,
                    hf_ref, hb_ref, c_ref, n_ref, m_ref):
    L = ML_CHUNK

    @pl.when(pl.program_id(0) == 0)
    def _():
        c_ref[...] = jnp.zeros_like(c_ref)
        n_ref[...] = jnp.zeros_like(n_ref)
        m_ref[...] = jnp.zeros_like(m_ref)

    row = lax.broadcasted_iota(jnp.int32, (L, L), 0)
    colm = lax.broadcasted_iota(jnp.int32, (L, L), 1)
    lower = colm <= row
    upper = colm >= row
    t_lower = jnp.where(lower, 1.0, 0.0).astype(BF16)
    t_upper = jnp.where(upper, 1.0, 0.0).astype(BF16)

    def tri_left(t, x):
        hi, mid, lo = _split3(x)
        return _dot(t, hi) + _dot(t, mid) + _dot(t, lo)

    def tri_right(x, t):
        hi, mid, lo = _split3(x)
        return _dot(hi, t) + _dot(mid, t) + _dot(lo, t)

    dirs = (
        (qf_ref, kf_ref, vf_ref, gcf_ref, grf_ref, hf_ref, t_lower, t_upper, lower, L - 1),
        (qb_ref, kb_ref, vb_ref, gcb_ref, grb_ref, hb_ref, t_upper, t_lower, upper, 0),
    )
    for dd, (q_ref, k_ref, v_ref, gc_ref, gr_ref, h_ref, t_col, t_row, mask, last) in enumerate(dirs):
        gcol = gc_ref[...]
        grow = gr_ref[...]
        bcol = tri_left(t_col, gcol)
        brow = tri_right(grow, t_row)
        for h in range(ML_HEADS):
            ch = dd * ML_HEADS + h
            li = dd * 2 * ML_HEADS + h
            lf = li + ML_HEADS
            sl = slice(h * ML_HP, (h + 1) * ML_HP)
            q = q_ref[:, sl]
            k = k_ref[:, sl]
            v = v_ref[:, sl]
            b_c = bcol[:, lf:lf + 1]
            b_r = brow[lf:lf + 1, :]
            li_c = gcol[:, li:li + 1]
            li_r = grow[li:li + 1, :]
            m_prev = m_ref[ch:ch + 1, 0:1]
            c_prev = c_ref[ch]
            n_prev = n_ref[ch]

            logw = jnp.where(mask, b_c - b_r + li_r, NEG)
            inter = b_c + m_prev
            m = jnp.maximum(jnp.max(logw, axis=-1, keepdims=True), inter)
            a = jnp.exp(logw - m) * _dot_nt(q, k)
            decay = jnp.exp(inter - m)
            num = _dot(a.astype(BF16), v) + decay * _dot(q, c_prev.astype(BF16))
            qn = jnp.sum(q.astype(F32) * n_prev, axis=-1, keepdims=True)
            den = jnp.sum(a, axis=-1, keepdims=True) + decay * qn
            h_ref[:, sl] = num / jnp.maximum(jnp.abs(den), jnp.exp(-m))

            m_new = m[last:last + 1, :]
            b_last = b_c[last:last + 1, :]
            g = jnp.exp(b_last - b_c + li_c - m_new)
            cd = jnp.exp(b_last + m_prev - m_new)
            kg = k.astype(F32) * g
            c_ref[ch] = cd * c_prev + _dot_tn(kg.astype(BF16), v)
            n_ref[ch] = cd * n_prev + jnp.sum(kg, axis=0, keepdims=True)
            m_ref[ch:ch + 1, :] = jnp.broadcast_to(m_new, (1, LANES))


def _ml_scan(q, k, main, gcol, grow):
    S = q.shape[0]
    L = ML_CHUNK
    nc = S // L
    fwd = lambda j: (j, 0)
    bwd = lambda j: (nc - 1 - j, 0)
    blk = lambda f: pl.BlockSpec((L, ML_WP), f)
    vblk = lambda f: pl.BlockSpec((L, ML_WP), lambda j: (f(j)[0], 4))
    out = jax.ShapeDtypeStruct((S, ML_WP), F32)
    return pl.pallas_call(
        _ml_scan_kernel,
        grid=(nc,),
        in_specs=[
            blk(fwd), blk(fwd), vblk(fwd), pl.BlockSpec((L, LANES), fwd), pl.BlockSpec((LANES, L), lambda j: (0, j)),
            blk(bwd), blk(bwd), vblk(bwd), pl.BlockSpec((L, LANES), bwd),
            pl.BlockSpec((LANES, L), lambda j: (0, nc - 1 - j)),
        ],
        out_specs=[blk(fwd), blk(bwd)],
        out_shape=[out, out],
        scratch_shapes=[
            pltpu.VMEM((2 * ML_HEADS, ML_HP, ML_HP), F32),
            pltpu.VMEM((2 * ML_HEADS, 1, ML_HP), F32),
            pltpu.VMEM((2 * ML_HEADS, LANES), F32),
        ],
        compiler_params=_cparams(("arbitrary",)),
        name="ml_scan",
    )(q, k, main, gcol, grow, q, k, main, gcol, grow)


def _outproj_kernel(x_ref, ymla_ref, o1_ref, o2_ref, o3_ref, s1_ref, s2_ref, s3_ref, hf_ref, hb_ref, mo_ref,
                    gout_ref, e_ref, w1_ref, w2_ref, w3_ref, out_ref, so_ref, sst_ref):
    tm = x_ref.shape[0]

    def token_order(src_ref, scr_ref):
        d, _, width = src_ref.shape
        if d == 1:
            return src_ref[0].astype(F32)
        for b in range(width // LANES):
            for r in range(d):
                scr_ref[b, pl.ds(r, tm // d, stride=d), :] = src_ref[r, :, b * LANES:(b + 1) * LANES].astype(F32)
        return jnp.concatenate([scr_ref[b] for b in range(width // LANES)], axis=1)

    sts = [token_order(s_ref, sst_ref.at[i]) for i, s_ref in enumerate((s1_ref, s2_ref, s3_ref))]
    m_all = jnp.maximum(jnp.maximum(sts[0], sts[1]), sts[2])
    ws = [jnp.exp2(st - m_all) * pltpu.roll(st, LANES // 2, 1) for st in sts]
    inv = 1.0 / (ws[0] + ws[1] + ws[2])
    lane = _lane(inv.shape)
    ydil = jnp.zeros((tm, DIL_W), F32)
    for i, (w, o_ref) in enumerate(zip(ws, (o1_ref, o2_ref, o3_ref))):
        alpha = jnp.where(lane < DIL_HEADS, w * inv, 0.0)
        hi, mid, lo = _split3(alpha)
        e = e_ref[...]
        ydil = ydil + (_dot(hi, e) + _dot(mid, e) + _dot(lo, e)) * token_order(o_ref, so_ref.at[i])

    hs = hf_ref[...] + hb_ref[...]
    yml = []
    for h in range(ML_HEADS):
        sl = slice(h * ML_HP, (h + 1) * ML_HP)
        hh = hs[:, sl]
        ss = jnp.sum(hh * hh, axis=-1, keepdims=True) * (1.0 / ML_HD)
        yml.append(hh * lax.rsqrt(ss + EPS) * gout_ref[:, sl] * jax.nn.sigmoid(mo_ref[:, sl].astype(F32)))
    yml = jnp.concatenate(yml, axis=-1)

    out_ref[...] = (x_ref[...] + _dot(ymla_ref[...], w1_ref[...]) + _dot(ydil.astype(BF16), w2_ref[...])
                    + _dot(yml.astype(BF16), w3_ref[...]))


def _outproj(x, ymla, dil_o, dil_st, hf, hb, main, gout, e, w1, w2, w3, tm=512):
    S, D = x.shape
    row = lambda w: pl.BlockSpec((tm, w), lambda i: (i, 0))
    full = lambda shape: pl.BlockSpec(shape, lambda i: (0, 0))
    res = lambda w: [pl.BlockSpec((d, tm // d, w), lambda i: (0, i, 0)) for _, d in DIL_PATTERNS]
    return pl.pallas_call(
        _outproj_kernel,
        grid=(S // tm,),
        in_specs=[
            row(D), row(MLA_HEADS * MLA_V), *res(DIL_W), *res(LANES),
            row(ML_WP), row(ML_WP), pl.BlockSpec((tm, ML_WP), lambda i: (i, 5)),
            full((1, ML_WP)), full((LANES, DIL_W)), full((MLA_HEADS * MLA_V, D)), full((DIL_W, D)), full((ML_WP, D)),
        ],
        out_specs=row(D),
        out_shape=jax.ShapeDtypeStruct((S, D), F32),
        scratch_shapes=[pltpu.VMEM((len(DIL_PATTERNS), DIL_W // LANES, tm, LANES), F32),
                        pltpu.VMEM((len(DIL_PATTERNS), 1, tm, LANES), F32)],
        compiler_params=_cparams(("parallel",)),
        name="outproj",
    )(x, ymla, *dil_o, *dil_st, hf, hb, main, gout, e, w1, w2, w3)


def _router_kernel(x_ref, g_ref, wr_ref, br_ref, xn_ref, route_ref):
    x = x_ref[...]
    xn = x * lax.rsqrt(jnp.mean(x * x, axis=-1, keepdims=True) + EPS) * g_ref[...]
    xn_ref[...] = _pack_pairs(xn)
    x1, x2, x3 = _split3(xn)
    w1 = wr_ref[0]
    w2 = wr_ref[1]
    w3 = wr_ref[2]
    logits = (_dot(x1, w1) + (_dot(x1, w2) + _dot(x2, w1)) + (_dot(x1, w3) + _dot(x2, w2) + _dot(x3, w1))
              + br_ref[...])
    lane = _lane(logits.shape)
    big = jnp.int32(4 * LANES)

    def first_max(mask):
        v = jnp.max(jnp.where(mask, logits, NEG), axis=-1, keepdims=True)
        idx = jnp.min(jnp.where(mask & (logits == v), lane, big), axis=-1, keepdims=True)
        return v, idx

    gmask = lane < N_GROUPS
    gmax, grp = first_max(gmask)
    p_grp = 1.0 / jnp.sum(jnp.where(gmask, jnp.exp(logits - gmax), 0.0), axis=-1, keepdims=True)
    e_lo = N_GROUPS + grp * EXPERTS_PER_GROUP
    emask = (lane >= e_lo) & (lane < e_lo + EXPERTS_PER_GROUP)
    v1, i1 = first_max(emask)
    v2, i2 = first_max(emask & (lane != i1))
    t = jnp.exp(v2 - v1)
    w_1 = p_grp / (1.0 + t)
    w_2 = p_grp * t / (1.0 + t)
    e_1 = (i1 - N_GROUPS).astype(F32)
    e_2 = (i2 - N_GROUPS).astype(F32)
    route_ref[...] = jnp.where(lane == 0, e_1, jnp.where(lane == 1, e_2,
                               jnp.where(lane == 2, w_1, jnp.where(lane == 3, w_2, 0.0))))


def _router(x, g, wr, br, tm=512):
    S, D = x.shape
    return pl.pallas_call(
        _router_kernel,
        grid=(S // tm,),
        in_specs=[
            pl.BlockSpec((tm, D), lambda i: (i, 0)),
            pl.BlockSpec((1, D), lambda i: (0, 0)),
            pl.BlockSpec((3, D, LANES), lambda i: (0, 0, 0)),
            pl.BlockSpec((1, LANES), lambda i: (0, 0)),
        ],
        out_specs=[pl.BlockSpec((tm, D // 2), lambda i: (i, 0)), pl.BlockSpec((tm, LANES), lambda i: (i, 0))],
        out_shape=[jax.ShapeDtypeStruct((S, D // 2), jnp.uint32), jax.ShapeDtypeStruct((S, LANES), F32)],
        compiler_params=_cparams(("parallel",)),
        name="router",
    )(x, g, wr, br)


def _pack_pairs(x):
    n = x.shape[1] // 2
    lo = lax.bitcast_convert_type(x[:, :n].astype(BF16).astype(F32), jnp.uint32)
    hi = lax.bitcast_convert_type(x[:, n:].astype(BF16).astype(F32), jnp.uint32)
    return (hi & jnp.uint32(0xFFFF0000)) | (lo >> 16)


def _unpack_pairs(w):
    lo = lax.bitcast_convert_type(w << 16, F32)
    hi = lax.bitcast_convert_type(w & jnp.uint32(0xFFFF0000), F32)
    return jnp.concatenate([lo, hi], axis=1)


def _route_plan(route, tile, n_tiles):
    S = route.shape[0]
    P = 2 * S
    hp = lax.Precision.HIGHEST
    e = route[:, :2].astype(jnp.int32).reshape(P)
    oh = (e[:, None] == jnp.arange(N_EXPERTS, dtype=jnp.int32)[None, :]).astype(F32)
    nb = P // LANES
    ohb = oh.reshape(nb, LANES, N_EXPERTS)
    within = jnp.einsum("ts,bse->bte", jnp.tril(jnp.ones((LANES, LANES), F32)), ohb, precision=hp)
    tot = within[:, -1, :]
    before = jnp.einsum("cb,be->ce", jnp.tril(jnp.ones((nb, nb), F32), -1), tot, precision=hp)
    rank = jnp.sum((within + before[:, None, :]) * ohb, axis=-1).reshape(P) - 1.0
    counts = jnp.sum(tot, axis=0).astype(jnp.int32)
    tiles_e = (counts + tile - 1) // tile
    tile_end = jnp.cumsum(tiles_e)
    row_start = ((tile_end - tiles_e) * tile).astype(F32)
    dest = (jnp.sum(oh * row_start[None, :], axis=1) + rank).astype(jnp.int32)
    n_active = tile_end[-1]
    tidx = jnp.minimum(jnp.arange(n_tiles, dtype=jnp.int32), n_active - 1)
    tile_expert = jnp.sum((tidx[:, None] >= tile_end[None, :]).astype(jnp.int32), axis=1)
    return dest, tile_expert.astype(jnp.int32), n_active.reshape(1).astype(jnp.int32)


def _row_copies(n, make, unroll=8):
    def start(t, c):
        make(t, 0).start(priority=0)
        make(t, 1).start(priority=1)
        return c

    def wait(t, c):
        make(t, 0).wait()
        make(t, 1).wait()
        return c

    lax.fori_loop(0, n, start, 0, unroll=unroll)
    lax.fori_loop(0, n, wait, 0, unroll=unroll)


def _dispatch_kernel(dest_ref, xn_ref, xs_in_ref, xs_ref, sem):
    del xs_in_ref
    tm = xn_ref.shape[0]
    _row_copies(tm, lambda t, slot: pltpu.make_async_copy(
        xn_ref.at[pl.ds(t, 1)], xs_ref.at[pl.ds(dest_ref[0, 2 * t + slot], 1)], sem))


def _dispatch(xn, dest, n_rows, tm=512):
    S, W = xn.shape
    return pl.pallas_call(
        _dispatch_kernel,
        grid=(S // tm,),
        in_specs=[
            pl.BlockSpec((None, 1, 2 * tm), lambda i: (i, 0, 0), memory_space=pltpu.SMEM),
            pl.BlockSpec((tm, W), lambda i: (i, 0)),
            pl.BlockSpec(memory_space=pl.ANY),
        ],
        out_specs=pl.BlockSpec(memory_space=pl.ANY),
        out_shape=jax.ShapeDtypeStruct((n_rows, W), xn.dtype),
        scratch_shapes=[pltpu.SemaphoreType.DMA(())],
        input_output_aliases={2: 0},
        compiler_params=_cparams(("arbitrary",)),
        name="moe_dispatch",
    )(dest.reshape(S // tm, 1, 2 * tm), xn, jnp.zeros((n_rows, W), xn.dtype))


def _experts_kernel(te_ref, na_ref, xs_ref, wg_ref, wu_ref, wd_ref, y_ref, wgb_ref, wub_ref, wdb_ref):
    i = pl.program_id(0)

    @pl.when(i < na_ref[0])
    def _():
        @pl.when((i == 0) | (te_ref[i] != te_ref[jnp.maximum(i - 1, 0)]))
        def _():
            wgb_ref[...] = wg_ref[...].astype(BF16)
            wub_ref[...] = wu_ref[...].astype(BF16)
            wdb_ref[...] = wd_ref[...].astype(BF16)

        x = _unpack_pairs(xs_ref[...]).astype(BF16)
        gate = _dot(x, wgb_ref[...])
        up = _dot(x, wub_ref[...])
        hid = (gate * jax.nn.sigmoid(gate) * up).astype(BF16)
        y_ref[...] = _pack_pairs(_dot(hid, wdb_ref[...]))

    @pl.when(i >= na_ref[0])
    def _():
        y_ref[...] = jnp.zeros_like(y_ref)


def _experts(xs, tile_expert, n_active, wg, wu, wd, layer, tile):
    R, W = xs.shape
    D = 2 * W
    row = lambda i, te, na: (jnp.minimum(i, na[0] - 1), 0)
    wsel = lambda i, te, na: (layer, te[i], 0, 0)
    return pl.pallas_call(
        _experts_kernel,
        grid_spec=pltpu.PrefetchScalarGridSpec(
            num_scalar_prefetch=2,
            grid=(R // tile,),
            in_specs=[
                pl.BlockSpec((tile, W), row),
                pl.BlockSpec((None, None, D, D_EXPERT), wsel),
                pl.BlockSpec((None, None, D, D_EXPERT), wsel),
                pl.BlockSpec((None, None, D_EXPERT, D), wsel),
            ],
            out_specs=pl.BlockSpec((tile, W), lambda i, te, na: (i, 0)),
            scratch_shapes=[pltpu.VMEM((D, D_EXPERT), BF16), pltpu.VMEM((D, D_EXPERT), BF16),
                            pltpu.VMEM((D_EXPERT, D), BF16)],
        ),
        out_shape=jax.ShapeDtypeStruct((R, W), jnp.uint32),
        compiler_params=_cparams(("arbitrary",)),
        name="moe_experts",
    )(tile_expert, n_active, xs, wg, wu, wd)


def _combine_kernel(dest_ref, x_ref, route_ref, y_ref, out_ref, ya_ref, yb_ref, sem):
    tm = x_ref.shape[0]
    bufs = (ya_ref, yb_ref)
    _row_copies(tm, lambda t, slot: pltpu.make_async_copy(
        y_ref.at[pl.ds(dest_ref[0, 2 * t + slot], 1)], bufs[slot].at[pl.ds(t, 1)], sem))
    route = route_ref[...]
    lane = _lane(route.shape)
    w_1 = jnp.sum(jnp.where(lane == 2, route, 0.0), axis=-1, keepdims=True)
    w_2 = jnp.sum(jnp.where(lane == 3, route, 0.0), axis=-1, keepdims=True)
    out_ref[...] = x_ref[...] + w_1 * _unpack_pairs(ya_ref[...]) + w_2 * _unpack_pairs(yb_ref[...])


def _combine(x, route, y, dest, tm=256):
    S, D = x.shape
    return pl.pallas_call(
        _combine_kernel,
        grid=(S // tm,),
        in_specs=[
            pl.BlockSpec((None, 1, 2 * tm), lambda i: (i, 0, 0), memory_space=pltpu.SMEM),
            pl.BlockSpec((tm, D), lambda i: (i, 0)),
            pl.BlockSpec((tm, LANES), lambda i: (i, 0)),
            pl.BlockSpec(memory_space=pl.ANY),
        ],
        out_specs=pl.BlockSpec((tm, D), lambda i: (i, 0)),
        out_shape=jax.ShapeDtypeStruct((S, D), F32),
        scratch_shapes=[pltpu.VMEM((tm, D // 2), jnp.uint32), pltpu.VMEM((tm, D // 2), jnp.uint32),
                        pltpu.SemaphoreType.DMA(())],
        compiler_params=_cparams(("arbitrary",)),
        name="moe_combine",
    )(dest.reshape(S // tm, 1, 2 * tm), x, route, y)


def _moe(x, xn, route, wg, wu, wd, layer, tile=256):
    S = x.shape[0]
    n_rows = 2 * S + N_EXPERTS * tile
    dest, tile_expert, n_active = _route_plan(route, tile, n_rows // tile)
    xs = _dispatch(xn, dest, n_rows)
    y = _experts(xs, tile_expert, n_active, wg, wu, wd, layer, tile)
    return _combine(x, route, y, dest)


def _pad_cols(w, width):
    return jnp.pad(w, ((0, 0), (0, width - w.shape[1])))


def _pad_heads(w):
    lead = w.shape[:-1]
    w = w.reshape(*lead, ML_HEADS, ML_HD)
    w = jnp.pad(w, [(0, 0)] * len(lead) + [(0, 0), (0, ML_HP - ML_HD)])
    return w.reshape(*lead, ML_WP)


def _t5_buckets(rel):
    half = N_BUCKETS // 2
    max_exact = half // 2
    n = np.abs(rel)
    large = max_exact + (np.log(np.maximum(n, 1) / max_exact) / np.log(BUCKET_MAX_DIST / max_exact)
                         * (half - max_exact)).astype(np.int32)
    large = np.minimum(large, half - 1)
    return (rel > 0).astype(np.int32) * half + np.where(n < max_exact, n, large).astype(np.int32)


def _band_bias(rel_bias, d, tq=128):
    win = tq + 2 * DIL_R
    offs = (np.arange(2 * DIL_R + 1) - DIL_R) * d
    t = rel_bias.astype(F32)[jnp.asarray(_t5_buckets(offs))] * LOG2E
    v = jnp.concatenate([t, jnp.full((win + 1 - t.shape[0], DIL_HEADS), NEG, F32)], axis=0)
    b = jnp.tile(v, (tq, 1))[:tq * win].reshape(tq, win, DIL_HEADS)
    return jnp.transpose(b, (2, 0, 1))


def _layer_weights(l, p):
    w_in = p["w_in"][l]
    off = np.cumsum((0, MLA_Q_RANK, MLA_KV_RANK, MLA_ROPE, DIL_W, DIL_W, DIL_W, ML_W, ML_W, ML_W, 4 * ML_HEADS))
    col = lambda a, b: w_in[:, off[a]:off[b]]
    w_main = jnp.concatenate([
        col(3, 6),
        _pad_cols(col(0, 3), DIL_W),
        _pad_cols(col(6, 7), ML_WP),
        _pad_heads(col(7, 8)), _pad_heads(col(8, 9)),
    ], axis=1).astype(BF16)
    w_gate = _pad_cols(col(9, 10), LANES).astype(BF16)

    wq = p["mla_w_uq"][l].reshape(MLA_Q_RANK, MLA_HEADS, MLA_QK)
    wq = jnp.pad(wq, ((0, 0), (0, 0), (0, LANES - MLA_QK))).reshape(MLA_Q_RANK, MLA_HEADS * LANES).astype(BF16)
    wkv = p["mla_w_ukv"][l].reshape(MLA_KV_RANK, MLA_HEADS, MLA_NOPE + MLA_V)
    padh = lambda w: jnp.pad(w, ((0, 0), (0, 0), (0, LANES - w.shape[-1]))).reshape(
        MLA_KV_RANK, MLA_HEADS * LANES).astype(BF16)
    wk = padh(wkv[..., :MLA_NOPE])
    wv = padh(wkv[..., MLA_NOPE:])
    vone = jnp.tile((jnp.arange(LANES) == MLA_V).astype(F32), MLA_HEADS)[None, :]
    lane_pad = lambda g: jnp.pad(g, (0, LANES - g.shape[0]))[None, :]

    def block_diag(w):
        out = jnp.zeros((ML_WP, ML_WP), F32)
        for h in range(ML_HEADS):
            out = out.at[h * ML_HD:(h + 1) * ML_HD, h * ML_HP:h * ML_HP + ML_HD].set(w[h])
        return out
    wqk = jnp.concatenate([block_diag(p["ml_w_q"][l]), block_diag(p["ml_w_k"][l]) * (ML_HD ** -0.5)],
                          axis=1).astype(BF16)
    cw = jnp.pad(p["ml_conv_w"][l], ((0, HALO - ML_CONV), (0, ML_WP - ML_W)))
    cb = _pad_cols(p["ml_conv_b"][l][None, :], ML_WP)
    gb = _pad_cols(p["ml_gate_bias"][l].reshape(1, 4 * ML_HEADS), LANES)

    w_out = p["w_out"][l]
    n_mla = MLA_HEADS * MLA_V
    w3 = w_out[n_mla + DIL_W:].reshape(ML_HEADS, ML_HD, D_MODEL)
    w3 = jnp.pad(w3, ((0, 0), (0, ML_HP - ML_HD), (0, 0))).reshape(ML_WP, D_MODEL)

    wr = jnp.concatenate([p["router_group_w"][l], p["router_expert_w"][l]], axis=1)
    wr = jnp.stack(_split3(_pad_cols(wr, LANES)))
    br = _pad_cols(jnp.concatenate([p["router_group_b"][l], p["router_expert_b"][l]])[None, :], LANES)
    return dict(
        norm_mix=p["norm_mix"][l][None, :], w_main=w_main, w_gate=w_gate,
        gcq=p["mla_norm_cq"][l][None, :], wq=wq, gckv=p["mla_norm_ckv"][l][None, :], wk=wk, wv=wv, vone=vone,
        gq=lane_pad(p["mla_q_norm"][l]), gk=lane_pad(p["mla_k_norm"][l]),
        dgq=jnp.tile(p["dil_q_norm"][l], 2)[None, :], dgk=jnp.tile(p["dil_k_norm"][l], 2)[None, :],
        cw=cw, cb=cb, wqk=wqk, gb=gb, gout=_pad_heads(p["ml_out_norm"][l][None, :]),
        w1=w_out[:n_mla].astype(BF16), w2=w_out[n_mla:n_mla + DIL_W].astype(BF16), w3=w3.astype(BF16),
        norm_ffn=p["norm_ffn"][l][None, :], wr=wr, br=br,
    )


def kernel(x, positions, rel_bias, norm_mix, w_in, mla_norm_cq, mla_w_uq, mla_norm_ckv, mla_w_ukv, mla_q_norm, mla_k_norm, dil_q_norm, dil_k_norm, ml_conv_w, ml_conv_b, ml_w_q, ml_w_k, ml_gate_bias, ml_out_norm, w_out, norm_ffn, router_group_w, router_group_b, router_expert_w, router_expert_b, moe_w_gate, moe_w_up, moe_w_down):
    params = dict(norm_mix=norm_mix, w_in=w_in, mla_norm_cq=mla_norm_cq, mla_w_uq=mla_w_uq,
                  mla_norm_ckv=mla_norm_ckv, mla_w_ukv=mla_w_ukv, mla_q_norm=mla_q_norm, mla_k_norm=mla_k_norm,
                  dil_q_norm=dil_q_norm, dil_k_norm=dil_k_norm, ml_conv_w=ml_conv_w, ml_conv_b=ml_conv_b,
                  ml_w_q=ml_w_q, ml_w_k=ml_w_k, ml_gate_bias=ml_gate_bias, ml_out_norm=ml_out_norm, w_out=w_out,
                  norm_ffn=norm_ffn, router_group_w=router_group_w, router_group_b=router_group_b,
                  router_expert_w=router_expert_w, router_expert_b=router_expert_b, moe_w_gate=moe_w_gate,
                  moe_w_up=moe_w_up, moe_w_down=moe_w_down)
    B, S, D = x.shape
    assert B == 1 and D == D_MODEL
    depth = w_in.shape[0]
    xs = x.reshape(S, D)
    pos = positions.reshape(S, 1)

    half = MLA_ROPE // 2
    inv_freq = ROPE_THETA ** (-jnp.arange(half, dtype=F32) / half)
    invf = jnp.zeros((LANES,), F32).at[MLA_NOPE:MLA_NOPE + half].set(inv_freq)
    invf = invf.at[MLA_NOPE + half:MLA_QK].set(inv_freq)[None, :]
    biases = [_band_bias(rel_bias, d) for _, d in DIL_PATTERNS]
    expand = (jnp.arange(LANES)[:, None] == (jnp.arange(DIL_W) // DIL_HD)[None, :]).astype(BF16)

    for l in range(depth):
        w = _layer_weights(l, params)
        main, gates = _inproj(xs, w["norm_mix"], w["w_main"], w["w_gate"])
        q, k, v = _mla_prep(main, pos, invf, w["gcq"], w["wq"], w["gckv"], w["wk"], w["wv"], w["vone"],
                            w["gq"], w["gk"])
        ymla = _mla_attn(q, k, v)
        dqkv = _dil_norm(main, w["dgq"], w["dgk"])
        dil = [_dil_band(*dqkv[3 * i:3 * i + 3], b) for i, b in enumerate(biases)]
        mq, mk, gcol, grow = _ml_prep(main, gates, w["cw"], w["cb"], w["wqk"], w["gb"])
        hf, hb = _ml_scan(mq, mk, main, gcol, grow)
        xs = _outproj(xs, ymla, [o for o, _ in dil], [s for _, s in dil], hf, hb, main, w["gout"], expand,
                      w["w1"], w["w2"], w["w3"])
        xn, route = _router(xs, w["norm_ffn"], w["wr"], w["br"])
        xs = _moe(xs, xn, route, moe_w_gate, moe_w_up, moe_w_down, l)
    return xs.reshape(B, S, D)
```

```python
import functools

import numpy as np
import jax
import jax.numpy as jnp
from jax import lax
from jax.experimental import pallas as pl
from jax.experimental.pallas import tpu as pltpu

F32 = jnp.float32
BF16 = jnp.bfloat16

D_MODEL = 2048
EPS = 1e-6
NEG = -1e30
LOG2E = 1.4426950408889634

MLA_HEADS = 8
MLA_Q_RANK = 384
MLA_KV_RANK = 128
MLA_NOPE = 64
MLA_ROPE = 32
MLA_V = 64
MLA_QK = MLA_NOPE + MLA_ROPE
MLA_VROWS = 80
ROPE_THETA = 10000.0

DIL_HEADS = 12
DIL_HD = 64
DIL_W = DIL_HEADS * DIL_HD
DIL_PATTERNS = ((128, 1), (512, 4), (2048, 16))
DIL_R = 64
N_BUCKETS = 32
BUCKET_MAX_DIST = 1024

ML_HEADS = 4
ML_HD = 192
ML_HP = 256
ML_W = ML_HEADS * ML_HD
ML_WP = ML_HEADS * ML_HP
ML_CONV = 5
ML_CHUNK = 256

N_GROUPS = 4
EXPERTS_PER_GROUP = 4
N_EXPERTS = 16
D_EXPERT = 512

LANES = 128
HALO = 8
MAIN_W = 6144
VMEM_LIMIT = 56 * 1024 * 1024


def _cparams(sem):
    return pltpu.CompilerParams(dimension_semantics=sem, vmem_limit_bytes=VMEM_LIMIT)


def _wspec(layer, *shape):
    zeros = (0,) * len(shape)
    return pl.BlockSpec((None,) + shape, lambda *_: (layer,) + zeros)


def _lane(shape):
    return lax.broadcasted_iota(jnp.int32, shape, len(shape) - 1)


def _split3(x):
    hi = x.astype(BF16)
    r1 = x - hi.astype(F32)
    mid = r1.astype(BF16)
    lo = (r1 - mid.astype(F32)).astype(BF16)
    return hi, mid, lo


def _dot(a, b):
    return jnp.dot(a, b, preferred_element_type=F32)


def _dot_nt(a, b):
    return lax.dot_general(a, b, (((1,), (1,)), ((), ())), preferred_element_type=F32)


def _dot_tn(a, b):
    return lax.dot_general(a, b, (((0,), (0,)), ((), ())), preferred_element_type=F32)


def _inproj_kernel(x_ref, g_ref, w_ref, wg_ref, o_ref, og_ref, xn_ref):
    @pl.when(pl.program_id(1) == 0)
    def _():
        x = x_ref[...]
        ms = jnp.mean(x * x, axis=-1, keepdims=True)
        xn = (x * lax.rsqrt(ms + EPS) * g_ref[...]).astype(BF16)
        xn_ref[...] = xn
        og_ref[...] = _dot(xn, wg_ref[...])

    o_ref[...] = _dot(xn_ref[...], w_ref[...]).astype(o_ref.dtype)


def _inproj(x, g, w_main, w_gate, layer, tm=1024, tn=768):
    S, D = x.shape
    N = w_main.shape[-1]
    return pl.pallas_call(
        _inproj_kernel,
        grid=(S // tm, N // tn),
        in_specs=[
            pl.BlockSpec((tm, D), lambda i, j: (i, 0)),
            _wspec(layer, 1, D),
            pl.BlockSpec((None, D, tn), lambda i, j: (layer, 0, j)),
            _wspec(layer, D, LANES),
        ],
        out_specs=[
            pl.BlockSpec((tm, tn), lambda i, j: (i, j)),
            pl.BlockSpec((tm, LANES), lambda i, j: (i, 0)),
        ],
        out_shape=[jax.ShapeDtypeStruct((S, N), BF16), jax.ShapeDtypeStruct((S, LANES), F32)],
        scratch_shapes=[pltpu.VMEM((tm, D), BF16)],
        compiler_params=_cparams(("parallel", "arbitrary")),
        name="inproj",
    )(x, g, w_main, w_gate)


def _rope_tables(pos_ref, invf_ref):
    ang = pos_ref[...].astype(F32) * invf_ref[...]
    lane = _lane(ang.shape)
    cos = jnp.cos(ang)
    sin = jnp.sin(ang)
    rope_a = (lane >= MLA_NOPE) & (lane < MLA_NOPE + MLA_ROPE // 2)
    rope_b = (lane >= MLA_NOPE + MLA_ROPE // 2) & (lane < MLA_QK)
    c = jnp.where(lane < MLA_NOPE, 1.0, jnp.where(lane < MLA_QK, cos, 0.0))
    s1 = jnp.where(rope_a, -sin, 0.0)
    s2 = jnp.where(rope_b, sin, 0.0)
    return c, s1, s2


def _rope(x, tabs):
    c, s1, s2 = tabs
    half = MLA_ROPE // 2
    return x * c + pltpu.roll(x, LANES - half, 1) * s1 + pltpu.roll(x, half, 1) * s2


def _mla_prep_kernel(in_ref, pos_ref, invf_ref, gcq_ref, wq_ref, gckv_ref, wk_ref, wv_ref, vone_ref,
                     gq_ref, gk_ref, q_ref, k_ref, vt_ref):
    xin = in_ref[...].astype(F32)
    cq = xin[:, :MLA_Q_RANK]
    ckv = xin[:, MLA_Q_RANK:MLA_Q_RANK + MLA_KV_RANK]
    kr_blk = xin[:, MLA_Q_RANK + MLA_KV_RANK:MLA_Q_RANK + MLA_KV_RANK + LANES]

    cqn = (cq * lax.rsqrt(jnp.mean(cq * cq, axis=-1, keepdims=True) + EPS) * gcq_ref[...]).astype(BF16)
    ckvn = (ckv * lax.rsqrt(jnp.mean(ckv * ckv, axis=-1, keepdims=True) + EPS) * gckv_ref[...]).astype(BF16)
    q = _dot(cqn, wq_ref[...])
    kn = _dot(ckvn, wk_ref[...])
    v = _dot(ckvn, wv_ref[...]) + vone_ref[...]
    vt_ref[0] = v.T.astype(vt_ref.dtype)

    tabs = _rope_tables(pos_ref, invf_ref)
    lane = _lane(kr_blk.shape)
    kr = jnp.where((lane >= MLA_NOPE) & (lane < MLA_QK), pltpu.roll(kr_blk, MLA_NOPE, 1), 0.0)
    kr = _rope(kr, tabs)

    q_scale = (MLA_QK ** -0.5) * LOG2E
    for h in range(MLA_HEADS):
        sl = slice(h * LANES, (h + 1) * LANES)
        qh = _rope(q[:, sl], tabs)
        ss = jnp.sum(qh * qh, axis=-1, keepdims=True) * (1.0 / MLA_QK)
        q_ref[:, sl] = (qh * lax.rsqrt(ss + EPS) * gq_ref[...] * q_scale).astype(q_ref.dtype)
        kh = kn[:, sl] + kr
        ss = jnp.sum(kh * kh, axis=-1, keepdims=True) * (1.0 / MLA_QK)
        k_ref[:, sl] = (kh * lax.rsqrt(ss + EPS) * gk_ref[...]).astype(k_ref.dtype)


def _mla_prep(main, pos, invf, gcq, wq, gckv, wk, wv, vone, gq, gk, layer, tm=512):
    S = main.shape[0]
    HW = MLA_HEADS * LANES
    full = lambda shape: pl.BlockSpec(shape, lambda i: (0, 0))
    lw = functools.partial(_wspec, layer)
    out = jax.ShapeDtypeStruct((S, HW), BF16)
    return pl.pallas_call(
        _mla_prep_kernel,
        grid=(S // tm,),
        in_specs=[
            pl.BlockSpec((tm, 768), lambda i: (i, 3)),
            pl.BlockSpec((tm, 1), lambda i: (i, 0)),
            full((1, LANES)), lw(1, MLA_Q_RANK), lw(MLA_Q_RANK, HW), lw(1, MLA_KV_RANK),
            lw(MLA_KV_RANK, HW), lw(MLA_KV_RANK, HW), full((1, HW)), lw(1, LANES), lw(1, LANES),
        ],
        out_specs=[pl.BlockSpec((tm, HW), lambda i: (i, 0))] * 2 + [pl.BlockSpec((1, HW, tm), lambda i: (i, 0, 0))],
        out_shape=[out, out, jax.ShapeDtypeStruct((S // tm, HW, tm), BF16)],
        compiler_params=_cparams(("parallel",)),
        name="mla_prep",
    )(main, pos, invf, gcq, wq, gckv, wk, wv, vone, gq, gk)


def _mla_attn_kernel(q_ref, k_ref, vt_ref, o_ref, sa_ref, sb_ref, *, tk, sub, unroll):
    nk = k_ref.shape[0] // tk
    units = [(i, hh) for i in range(q_ref.shape[0] // sub) for hh in range(2)]
    fold = 8
    s_refs = (sa_ref, sb_ref)

    def logits_chunk(u, c, mx):
        i, hh = units[u]
        sl = slice(hh * LANES, (hh + 1) * LANES)
        rows = pl.ds(pl.multiple_of(c * tk, tk), tk)
        s = _dot_nt(k_ref[rows, sl], q_ref[i * sub:(i + 1) * sub, sl])
        s_refs[u % 2][c] = s
        return jnp.maximum(mx, jnp.max(s.reshape(fold, tk // fold, sub), axis=0))

    def weigh_chunk(u, c, m, acc):
        _, hh = units[u]
        p = jnp.exp2(s_refs[u % 2][c] - m).astype(BF16)
        return acc + _dot(vt_ref[c, hh * LANES:hh * LANES + MLA_VROWS, :], p)

    mx0 = jnp.full((tk // fold, sub), NEG, F32)
    acc0 = jnp.zeros((MLA_VROWS, sub), F32)
    mx = lax.fori_loop(0, nk, lambda c, mx: logits_chunk(0, c, mx), mx0, unroll=unroll)
    for u, (i, hh) in enumerate(units):
        m = jnp.max(mx, axis=0, keepdims=True)
        if u + 1 < len(units):
            def body(c, carry, u=u, m=m):
                acc, mx = carry
                return weigh_chunk(u, c, m, acc), logits_chunk(u + 1, c, mx)
            acc, mx = lax.fori_loop(0, nk, body, (acc0, mx0), unroll=unroll)
        else:
            acc = lax.fori_loop(0, nk, lambda c, acc, u=u, m=m: weigh_chunk(u, c, m, acc), acc0, unroll=unroll)
        o_t = acc[:MLA_V] / acc[MLA_V:MLA_V + 1]
        o = jnp.concatenate([o_t, jnp.zeros_like(o_t)], axis=0).T
        o_ref[i * sub:(i + 1) * sub, hh * MLA_V:(hh + 1) * MLA_V] = o[:, :MLA_V].astype(o_ref.dtype)


def _mla_attn(q, k, vt, tq=2048, sub=256, unroll=8):
    S = q.shape[0]
    nk, _, tk = vt.shape
    return pl.pallas_call(
        functools.partial(_mla_attn_kernel, tk=tk, sub=sub, unroll=unroll),
        grid=(MLA_HEADS // 2, S // tq),
        in_specs=[
            pl.BlockSpec((tq, 2 * LANES), lambda h, i: (i, h)),
            pl.BlockSpec((S, 2 * LANES), lambda h, i: (0, h)),
            pl.BlockSpec((nk, 2 * LANES, tk), lambda h, i: (0, h, 0)),
        ],
        out_specs=pl.BlockSpec((tq, 2 * MLA_V), lambda h, i: (i, h)),
        out_shape=jax.ShapeDtypeStruct((S, MLA_HEADS * MLA_V), BF16),
        scratch_shapes=[pltpu.VMEM((nk, tk, sub), F32)] * 2,
        compiler_params=_cparams(("parallel", "parallel")),
        name="mla_attn",
    )(q, k, vt)


def _dil_norm_kernel(q_ref, k_ref, v_ref, gq_ref, gk_ref, *refs):
    outs, (sq_ref, sk_ref, sv_ref) = refs[:9], refs[9:]
    tm = q_ref.shape[0]
    nb = DIL_W // LANES

    def norm(x_ref, g_ref, s_ref, scale):
        for b in range(nb):
            x = x_ref[:, b * LANES:(b + 1) * LANES].astype(F32)
            lo = _lane(x.shape) < DIL_HD
            sq = x * x
            ss_lo = jnp.sum(jnp.where(lo, sq, 0.0), axis=-1, keepdims=True)
            ss_hi = jnp.sum(jnp.where(lo, 0.0, sq), axis=-1, keepdims=True)
            r = lax.rsqrt(jnp.where(lo, ss_lo, ss_hi) * (1.0 / DIL_HD) + EPS)
            s_ref[b] = x * r * g_ref[...] * scale

    norm(q_ref, gq_ref, sq_ref, (DIL_HD ** -0.5) * LOG2E)
    norm(k_ref, gk_ref, sk_ref, 1.0)
    for b in range(nb):
        sv_ref[b] = v_ref[:, b * LANES:(b + 1) * LANES].astype(F32)
    for pi, (_, d) in enumerate(DIL_PATTERNS):
        for s_ref, o_ref in zip((sq_ref, sk_ref, sv_ref), outs[3 * pi:3 * pi + 3]):
            for b in range(nb):
                sl = slice(b * LANES, (b + 1) * LANES)
                if d == 1:
                    o_ref[0, :, sl] = s_ref[b].astype(o_ref.dtype)
                else:
                    for r in range(d):
                        o_ref[r, :, sl] = s_ref[b, pl.ds(r, tm // d, stride=d), :].astype(o_ref.dtype)


def _dil_norm(main, gq, gk, layer, tm=512):
    S = main.shape[0]
    out_specs, out_shape = [], []
    for _, d in DIL_PATTERNS:
        out_specs += [pl.BlockSpec((d, tm // d, DIL_W), lambda i: (0, i, 0))] * 3
        out_shape += [jax.ShapeDtypeStruct((d, S // d, DIL_W), BF16)] * 3
    return pl.pallas_call(
        _dil_norm_kernel,
        grid=(S // tm,),
        in_specs=[
            pl.BlockSpec((tm, DIL_W), lambda i: (i, 0)),
            pl.BlockSpec((tm, DIL_W), lambda i: (i, 1)),
            pl.BlockSpec((tm, DIL_W), lambda i: (i, 2)),
            _wspec(layer, 1, LANES),
            _wspec(layer, 1, LANES),
        ],
        out_specs=out_specs,
        out_shape=out_shape,
        scratch_shapes=[pltpu.VMEM((DIL_W // LANES, tm, LANES), F32)] * 3,
        compiler_params=_cparams(("parallel",)),
        name="dil_norm",
    )(main, main, main, gq, gk)


def _dil_band_kernel(q_ref, kp_ref, kc_ref, kn_ref, vp_ref, vc_ref, vn_ref, b_ref, o_ref, st_ref):
    ub = pl.program_id(1)
    nub = pl.num_programs(1)
    tq = q_ref.shape[0]
    win = tq + 2 * DIL_R
    col = _lane((1, win))
    prev_pen = jnp.where(ub > 0, 0.0, NEG)
    next_pen = jnp.where(ub < nub - 1, 0.0, NEG)
    edge = jnp.where(col < DIL_R, prev_pen, jnp.where(col >= DIL_R + tq, next_pen, 0.0))
    kwin = jnp.concatenate([kp_ref[...], kc_ref[...], kn_ref[...]], axis=0)
    vwin = jnp.concatenate([vp_ref[...], vc_ref[...], vn_ref[...]], axis=0)
    st_lane = _lane((tq, LANES))
    low = st_lane < DIL_HD
    st = jnp.zeros((tq, LANES), F32)
    zero = jnp.zeros((), BF16)
    for hp in range(DIL_HEADS // 2):
        sl = slice(hp * LANES, (hp + 1) * LANES)
        q2, k2, v2 = q_ref[:, sl], kwin[:, sl], vwin[:, sl]
        outs, invs = [], []
        for half in range(2):
            h = 2 * hp + half
            qh = jnp.where(low, q2, zero) if half == 0 else jnp.where(low, zero, q2)
            s = _dot_nt(qh, k2) + b_ref[h] + edge
            m = jnp.max(s, axis=-1, keepdims=True)
            p = jnp.exp2(s - m)
            den = jnp.sum(p, axis=-1, keepdims=True)
            outs.append(_dot(p.astype(BF16), v2))
            invs.append(1.0 / den)
            st = jnp.where(st_lane == h, m, jnp.where(st_lane == h + LANES // 2, den, st))
        o_ref[:, sl] = (jnp.where(low, outs[0], outs[1]) * jnp.where(low, invs[0], invs[1])).astype(o_ref.dtype)
    st_ref[...] = st


def _dil_band(q, k, v, bias, tq=128):
    d, sub, _ = q.shape
    hb = tq // DIL_R
    last = sub // DIL_R - 1
    cur = pl.BlockSpec((None, tq, DIL_W), lambda r, u: (r, u, 0))
    prev = pl.BlockSpec((None, DIL_R, DIL_W), lambda r, u: (r, jnp.maximum(u * hb - 1, 0), 0))
    nxt = pl.BlockSpec((None, DIL_R, DIL_W), lambda r, u: (r, jnp.minimum(u * hb + hb, last), 0))
    return pl.pallas_call(
        _dil_band_kernel,
        grid=(d, sub // tq),
        in_specs=[cur, prev, cur, nxt, prev, cur, nxt,
                  pl.BlockSpec((DIL_HEADS, tq, tq + 2 * DIL_R), lambda r, u: (0, 0, 0))],
        out_specs=[cur, pl.BlockSpec((None, tq, LANES), lambda r, u: (r, u, 0))],
        out_shape=[jax.ShapeDtypeStruct((d, sub, DIL_W), BF16), jax.ShapeDtypeStruct((d, sub, LANES), F32)],
        compiler_params=_cparams(("parallel", "parallel")),
        name=f"dil_band{d}",
    )(q, k, k, k, v, v, v, bias)


def _log_sigmoid(x):
    return jnp.minimum(x, 0.0) - jnp.log(1.0 + jnp.exp(-jnp.abs(x)))


def _ml_prep_kernel(up_ref, uc_ref, un_ref, cw_ref, cb_ref, wqk_ref, g_ref, gb_ref,
                    q_ref, k_ref, gcol_ref, grow_ref):
    i = pl.program_id(0)
    tm = uc_ref.shape[0]
    pm = jnp.where(i > 0, 1.0, 0.0)
    nm = jnp.where(i < pl.num_programs(0) - 1, 1.0, 0.0)
    xc = jnp.concatenate([up_ref[...].astype(F32) * pm, uc_ref[...].astype(F32), un_ref[...].astype(F32) * nm],
                         axis=0)
    acc = jnp.zeros((tm, ML_WP), F32) + cb_ref[...]
    for j in range(ML_CONV):
        off = HALO - ML_CONV // 2 + j
        acc = acc + xc[off:off + tm] * cw_ref[j:j + 1, :]
    u = (acc * jax.nn.sigmoid(acc)).astype(BF16)
    qk = _dot(u, wqk_ref[...])
    q_ref[...] = qk[:, :ML_WP].astype(q_ref.dtype)
    k_ref[...] = qk[:, ML_WP:].astype(k_ref.dtype)

    g = g_ref[...] + gb_ref[...]
    lane = _lane(g.shape)
    is_f = (lane % 8) >= ML_HEADS
    gcol = jnp.where(lane < 4 * ML_HEADS, jnp.where(is_f, _log_sigmoid(g), g), 0.0)
    gcol_ref[...] = gcol
    grow_ref[...] = gcol.T


def _ml_prep(main, gates, cw, cb, wqk, gb, layer, tm=512):
    S = main.shape[0]
    hb = tm // HALO
    last = S // HALO - 1
    return pl.pallas_call(
        _ml_prep_kernel,
        grid=(S // tm,),
        in_specs=[
            pl.BlockSpec((HALO, ML_WP), lambda i: (jnp.maximum(i * hb - 1, 0), 3)),
            pl.BlockSpec((tm, ML_WP), lambda i: (i, 3)),
            pl.BlockSpec((HALO, ML_WP), lambda i: (jnp.minimum(i * hb + hb, last), 3)),
            _wspec(layer, HALO, ML_WP), _wspec(layer, 1, ML_WP), _wspec(layer, ML_WP, 2 * ML_WP),
            pl.BlockSpec((tm, LANES), lambda i: (i, 0)),
            _wspec(layer, 1, LANES),
        ],
        out_specs=[
            pl.BlockSpec((tm, ML_WP), lambda i: (i, 0)),
            pl.BlockSpec((tm, ML_WP), lambda i: (i, 0)),
            pl.BlockSpec((tm, LANES), lambda i: (i, 0)),
            pl.BlockSpec((LANES, tm), lambda i: (0, i)),
        ],
        out_shape=[
            jax.ShapeDtypeStruct((S, ML_WP), BF16), jax.ShapeDtypeStruct((S, ML_WP), BF16),
            jax.ShapeDtypeStruct((S, LANES), F32), jax.ShapeDtypeStruct((LANES, S), F32),
        ],
        compiler_params=_cparams(("parallel",)),
        name="ml_prep",
    )(main, main, main, cw, cb, wqk, gates, gb)


def _ml_scan_kernel(qf_ref, kf_ref, vf_ref, gcf_ref, grf_ref, qb_ref, kb_ref, vb_ref, gcb_ref, grb_ref,
                    hf_ref, hb_ref, c_ref, n_ref, m_ref):
    L = ML_CHUNK

    @pl.when(pl.program_id(0) == 0)
    def _():
        c_ref[...] = jnp.zeros_like(c_ref)
        n_ref[...] = jnp.zeros_like(n_ref)
        m_ref[...] = jnp.zeros_like(m_ref)

    row = lax.broadcasted_iota(jnp.int32, (L, L), 0)
    colm = lax.broadcasted_iota(jnp.int32, (L, L), 1)
    lower = colm <= row
    upper = colm >= row
    t_lower = jnp.where(lower, 1.0, 0.0).astype(BF16)
    t_upper = jnp.where(upper, 1.0, 0.0).astype(BF16)

    def tri_left(t, x):
        hi, mid, lo = _split3(x)
        return _dot(t, hi) + _dot(t, mid) + _dot(t, lo)

    def tri_right(x, t):
        hi, mid, lo = _split3(x)
        return _dot(hi, t) + _dot(mid, t) + _dot(lo, t)

    dirs = (
        (qf_ref, kf_ref, vf_ref, gcf_ref, grf_ref, hf_ref, t_lower, t_upper, lower, L - 1),
        (qb_ref, kb_ref, vb_ref, gcb_ref, grb_ref, hb_ref, t_upper, t_lower, upper, 0),
    )
    for dd, (q_ref, k_ref, v_ref, gc_ref, gr_ref, h_ref, t_col, t_row, mask, last) in enumerate(dirs):
        gcol = gc_ref[...]
        grow = gr_ref[...]
        bcol = tri_left(t_col, gcol)
        brow = tri_right(grow, t_row)
        for h in range(ML_HEADS):
            ch = dd * ML_HEADS + h
            li = dd * 2 * ML_HEADS + h
            lf = li + ML_HEADS
            sl = slice(h * ML_HP, (h + 1) * ML_HP)
            q = q_ref[:, sl]
            k = k_ref[:, sl]
            v = v_ref[:, sl]
            b_c = bcol[:, lf:lf + 1]
            b_r = brow[lf:lf + 1, :]
            li_c = gcol[:, li:li + 1]
            li_r = grow[li:li + 1, :]
            m_prev = m_ref[ch:ch + 1, 0:1]
            c_prev = c_ref[ch]
            n_prev = n_ref[ch]

            logw = jnp.where(mask, b_c - b_r + li_r, NEG)
            inter = b_c + m_prev
            m = jnp.maximum(jnp.max(logw, axis=-1, keepdims=True), inter)
            a = jnp.exp(logw - m) * _dot_nt(q, k)
            decay = jnp.exp(inter - m)
            num = _dot(a.astype(BF16), v) + decay * _dot(q, c_prev.astype(BF16))
            qn = jnp.sum(q.astype(F32) * n_prev, axis=-1, keepdims=True)
            den = jnp.sum(a, axis=-1, keepdims=True) + decay * qn
            h_ref[:, sl] = num / jnp.maximum(jnp.abs(den), jnp.exp(-m))

            m_new = m[last:last + 1, :]
            b_last = b_c[last:last + 1, :]
            g = jnp.exp(b_last - b_c + li_c - m_new)
            cd = jnp.exp(b_last + m_prev - m_new)
            kg = k.astype(F32) * g
            c_ref[ch] = cd * c_prev + _dot_tn(kg.astype(BF16), v)
            n_ref[ch] = cd * n_prev + jnp.sum(kg, axis=0, keepdims=True)
            m_ref[ch:ch + 1, :] = jnp.broadcast_to(m_new, (1, LANES))


def _ml_scan(q, k, main, gcol, grow):
    S = q.shape[0]
    L = ML_CHUNK
    nc = S // L
    fwd = lambda j: (j, 0)
    bwd = lambda j: (nc - 1 - j, 0)
    blk = lambda f: pl.BlockSpec((L, ML_WP), f)
    vblk = lambda f: pl.BlockSpec((L, ML_WP), lambda j: (f(j)[0], 4))
    out = jax.ShapeDtypeStruct((S, ML_WP), F32)
    return pl.pallas_call(
        _ml_scan_kernel,
        grid=(nc,),
        in_specs=[
            blk(fwd), blk(fwd), vblk(fwd), pl.BlockSpec((L, LANES), fwd), pl.BlockSpec((LANES, L), lambda j: (0, j)),
            blk(bwd), blk(bwd), vblk(bwd), pl.BlockSpec((L, LANES), bwd),
            pl.BlockSpec((LANES, L), lambda j: (0, nc - 1 - j)),
        ],
        out_specs=[blk(fwd), blk(bwd)],
        out_shape=[out, out],
        scratch_shapes=[
            pltpu.VMEM((2 * ML_HEADS, ML_HP, ML_HP), F32),
            pltpu.VMEM((2 * ML_HEADS, 1, ML_HP), F32),
            pltpu.VMEM((2 * ML_HEADS, LANES), F32),
        ],
        compiler_params=_cparams(("arbitrary",)),
        name="ml_scan",
    )(q, k, main, gcol, grow, q, k, main, gcol, grow)


def _outproj_kernel(x_ref, ymla_ref, o1_ref, o2_ref, o3_ref, s1_ref, s2_ref, s3_ref, hf_ref, hb_ref, mo_ref,
                    gout_ref, e_ref, w1_ref, w2_ref, w3_ref, out_ref, so_ref, sst_ref):
    tm = x_ref.shape[0]

    def token_order(src_ref, scr_ref):
        d, _, width = src_ref.shape
        if d == 1:
            return src_ref[0].astype(F32)
        for b in range(width // LANES):
            for r in range(d):
                scr_ref[b, pl.ds(r, tm // d, stride=d), :] = src_ref[r, :, b * LANES:(b + 1) * LANES].astype(F32)
        return jnp.concatenate([scr_ref[b] for b in range(width // LANES)], axis=1)

    sts = [token_order(s_ref, sst_ref.at[i]) for i, s_ref in enumerate((s1_ref, s2_ref, s3_ref))]
    m_all = jnp.maximum(jnp.maximum(sts[0], sts[1]), sts[2])
    ws = [jnp.exp2(st - m_all) * pltpu.roll(st, LANES // 2, 1) for st in sts]
    inv = 1.0 / (ws[0] + ws[1] + ws[2])
    lane = _lane(inv.shape)
    ydil = jnp.zeros((tm, DIL_W), F32)
    for i, (w, o_ref) in enumerate(zip(ws, (o1_ref, o2_ref, o3_ref))):
        alpha = jnp.where(lane < DIL_HEADS, w * inv, 0.0)
        hi, mid, lo = _split3(alpha)
        e = e_ref[...]
        ydil = ydil + (_dot(hi, e) + _dot(mid, e) + _dot(lo, e)) * token_order(o_ref, so_ref.at[i])

    hs = hf_ref[...] + hb_ref[...]
    yml = []
    for h in range(ML_HEADS):
        sl = slice(h * ML_HP, (h + 1) * ML_HP)
        hh = hs[:, sl]
        ss = jnp.sum(hh * hh, axis=-1, keepdims=True) * (1.0 / ML_HD)
        yml.append(hh * lax.rsqrt(ss + EPS) * gout_ref[:, sl] * jax.nn.sigmoid(mo_ref[:, sl].astype(F32)))
    yml = jnp.concatenate(yml, axis=-1)

    out_ref[...] = (x_ref[...] + _dot(ymla_ref[...], w1_ref[...]) + _dot(ydil.astype(BF16), w2_ref[...])
                    + _dot(yml.astype(BF16), w3_ref[...]))


def _outproj(x, ymla, dil_o, dil_st, hf, hb, main, gout, e, w1, w2, w3, layer, tm=512):
    S, D = x.shape
    row = lambda w: pl.BlockSpec((tm, w), lambda i: (i, 0))
    full = lambda shape: pl.BlockSpec(shape, lambda i: (0, 0))
    res = lambda w: [pl.BlockSpec((d, tm // d, w), lambda i: (0, i, 0)) for _, d in DIL_PATTERNS]
    return pl.pallas_call(
        _outproj_kernel,
        grid=(S // tm,),
        in_specs=[
            row(D), row(MLA_HEADS * MLA_V), *res(DIL_W), *res(LANES),
            row(ML_WP), row(ML_WP), pl.BlockSpec((tm, ML_WP), lambda i: (i, 5)),
            _wspec(layer, 1, ML_WP), full((LANES, DIL_W)), _wspec(layer, MLA_HEADS * MLA_V, D),
            _wspec(layer, DIL_W, D), _wspec(layer, ML_WP, D),
        ],
        out_specs=row(D),
        out_shape=jax.ShapeDtypeStruct((S, D), F32),
        scratch_shapes=[pltpu.VMEM((len(DIL_PATTERNS), DIL_W // LANES, tm, LANES), F32),
                        pltpu.VMEM((len(DIL_PATTERNS), 1, tm, LANES), F32)],
        compiler_params=_cparams(("parallel",)),
        name="outproj",
    )(x, ymla, *dil_o, *dil_st, hf, hb, main, gout, e, w1, w2, w3)


def _pack_pairs(x):
    n = x.shape[1] // 2
    lo = lax.bitcast_convert_type(x[:, :n].astype(BF16).astype(F32), jnp.uint32)
    hi = lax.bitcast_convert_type(x[:, n:].astype(BF16).astype(F32), jnp.uint32)
    return (hi & jnp.uint32(0xFFFF0000)) | (lo >> 16)


def _unpack_pairs(w):
    lo = lax.bitcast_convert_type(w << 16, F32)
    hi = lax.bitcast_convert_type(w & jnp.uint32(0xFFFF0000), F32)
    return jnp.concatenate([lo, hi], axis=1)


def _router_kernel(x_ref, g_ref, wr_ref, br_ref, xn_ref, route_ref):
    x = x_ref[...]
    xn = x * lax.rsqrt(jnp.mean(x * x, axis=-1, keepdims=True) + EPS) * g_ref[...]
    xn_ref[...] = _pack_pairs(xn)
    x1, x2, x3 = _split3(xn)
    w1 = wr_ref[0]
    w2 = wr_ref[1]
    w3 = wr_ref[2]
    logits = (_dot(x1, w1) + (_dot(x1, w2) + _dot(x2, w1)) + (_dot(x1, w3) + _dot(x2, w2) + _dot(x3, w1))
              + br_ref[...])
    lane = _lane(logits.shape)
    big = jnp.int32(4 * LANES)

    def first_max(mask):
        v = jnp.max(jnp.where(mask, logits, NEG), axis=-1, keepdims=True)
        idx = jnp.min(jnp.where(mask & (logits == v), lane, big), axis=-1, keepdims=True)
        return v, idx

    gmask = lane < N_GROUPS
    gmax, grp = first_max(gmask)
    p_grp = 1.0 / jnp.sum(jnp.where(gmask, jnp.exp(logits - gmax), 0.0), axis=-1, keepdims=True)
    e_lo = N_GROUPS + grp * EXPERTS_PER_GROUP
    emask = (lane >= e_lo) & (lane < e_lo + EXPERTS_PER_GROUP)
    v1, i1 = first_max(emask)
    v2, i2 = first_max(emask & (lane != i1))
    t = jnp.exp(v2 - v1)
    w_1 = p_grp / (1.0 + t)
    w_2 = p_grp * t / (1.0 + t)
    e_1 = (i1 - N_GROUPS).astype(F32)
    e_2 = (i2 - N_GROUPS).astype(F32)
    route_ref[...] = jnp.where(lane == 0, e_1, jnp.where(lane == 1, e_2,
                               jnp.where(lane == 2, w_1, jnp.where(lane == 3, w_2, 0.0))))


def _router(x, g, wr, br, layer, tm=512):
    S, D = x.shape
    return pl.pallas_call(
        _router_kernel,
        grid=(S // tm,),
        in_specs=[
            pl.BlockSpec((tm, D), lambda i: (i, 0)),
            _wspec(layer, 1, D),
            _wspec(layer, 3, D, LANES),
            _wspec(layer, 1, LANES),
        ],
        out_specs=[pl.BlockSpec((tm, D // 2), lambda i: (i, 0)), pl.BlockSpec((tm, LANES), lambda i: (i, 0))],
        out_shape=[jax.ShapeDtypeStruct((S, D // 2), jnp.uint32), jax.ShapeDtypeStruct((S, LANES), F32)],
        compiler_params=_cparams(("parallel",)),
        name="router",
    )(x, g, wr, br)


def _route_plan(route, tile, n_tiles):
    S = route.shape[0]
    P = 2 * S
    hp = lax.Precision.HIGHEST
    e = route[:, :2].astype(jnp.int32).reshape(P)
    oh = (e[:, None] == jnp.arange(N_EXPERTS, dtype=jnp.int32)[None, :]).astype(F32)
    nb = P // LANES
    ohb = oh.reshape(nb, LANES, N_EXPERTS)
    within = jnp.einsum("ts,bse->bte", jnp.tril(jnp.ones((LANES, LANES), F32)), ohb, precision=hp)
    tot = within[:, -1, :]
    before = jnp.einsum("cb,be->ce", jnp.tril(jnp.ones((nb, nb), F32), -1), tot, precision=hp)
    rank = jnp.sum((within + before[:, None, :]) * ohb, axis=-1).reshape(P) - 1.0
    counts = jnp.sum(tot, axis=0).astype(jnp.int32)
    tiles_e = (counts + tile - 1) // tile
    tile_end = jnp.cumsum(tiles_e)
    row_start = ((tile_end - tiles_e) * tile).astype(F32)
    dest = (jnp.sum(oh * row_start[None, :], axis=1) + rank).astype(jnp.int32)
    n_active = tile_end[-1]
    tidx = jnp.minimum(jnp.arange(n_tiles, dtype=jnp.int32), n_active - 1)
    tile_expert = jnp.sum((tidx[:, None] >= tile_end[None, :]).astype(jnp.int32), axis=1)
    return dest, tile_expert.astype(jnp.int32), n_active.reshape(1).astype(jnp.int32)


def _row_copies(n, make, unroll=8):
    def start(t, c):
        make(t, 0).start(priority=0)
        make(t, 1).start(priority=1)
        return c

    def wait(t, c):
        make(t, 0).wait()
        make(t, 1).wait()
        return c

    lax.fori_loop(0, n, start, 0, unroll=unroll)
    lax.fori_loop(0, n, wait, 0, unroll=unroll)


def _dispatch_kernel(dest_ref, xn_ref, xs_in_ref, xs_ref, sem):
    del xs_in_ref
    tm = xn_ref.shape[0]
    _row_copies(tm, lambda t, slot: pltpu.make_async_copy(
        xn_ref.at[pl.ds(t, 1)], xs_ref.at[pl.ds(dest_ref[0, 2 * t + slot], 1)], sem))


def _dispatch(xn, dest, n_rows, tm=512):
    S, W = xn.shape
    return pl.pallas_call(
        _dispatch_kernel,
        grid=(S // tm,),
        in_specs=[
            pl.BlockSpec((None, 1, 2 * tm), lambda i: (i, 0, 0), memory_space=pltpu.SMEM),
            pl.BlockSpec((tm, W), lambda i: (i, 0)),
            pl.BlockSpec(memory_space=pl.ANY),
        ],
        out_specs=pl.BlockSpec(memory_space=pl.ANY),
        out_shape=jax.ShapeDtypeStruct((n_rows, W), xn.dtype),
        scratch_shapes=[pltpu.SemaphoreType.DMA(())],
        input_output_aliases={2: 0},
        compiler_params=_cparams(("arbitrary",)),
        name="moe_dispatch",
    )(dest.reshape(S // tm, 1, 2 * tm), xn, jnp.zeros((n_rows, W), xn.dtype))


def _experts_kernel(te_ref, na_ref, xs_ref, wg_ref, wu_ref, wd_ref, y_ref, wgb_ref, wub_ref, wdb_ref):
    i = pl.program_id(0)

    @pl.when(i < na_ref[0])
    def _():
        @pl.when((i == 0) | (te_ref[i] != te_ref[jnp.maximum(i - 1, 0)]))
        def _():
            wgb_ref[...] = wg_ref[...].astype(BF16)
            wub_ref[...] = wu_ref[...].astype(BF16)
            wdb_ref[...] = wd_ref[...].astype(BF16)

        x = _unpack_pairs(xs_ref[...]).astype(BF16)
        gate = _dot(x, wgb_ref[...])
        up = _dot(x, wub_ref[...])
        hid = (gate * jax.nn.sigmoid(gate) * up).astype(BF16)
        y_ref[...] = _pack_pairs(_dot(hid, wdb_ref[...]))

    @pl.when(i >= na_ref[0])
    def _():
        y_ref[...] = jnp.zeros_like(y_ref)


def _experts(xs, tile_expert, n_active, wg, wu, wd, layer, tile):
    R, W = xs.shape
    D = 2 * W
    row = lambda i, te, na: (jnp.minimum(i, na[0] - 1), 0)
    wsel = lambda i, te, na: (layer, te[i], 0, 0)
    return pl.pallas_call(
        _experts_kernel,
        grid_spec=pltpu.PrefetchScalarGridSpec(
            num_scalar_prefetch=2,
            grid=(R // tile,),
            in_specs=[
                pl.BlockSpec((tile, W), row),
                pl.BlockSpec((None, None, D, D_EXPERT), wsel),
                pl.BlockSpec((None, None, D, D_EXPERT), wsel),
                pl.BlockSpec((None, None, D_EXPERT, D), wsel),
            ],
            out_specs=pl.BlockSpec((tile, W), lambda i, te, na: (i, 0)),
            scratch_shapes=[pltpu.VMEM((D, D_EXPERT), BF16), pltpu.VMEM((D, D_EXPERT), BF16),
                            pltpu.VMEM((D_EXPERT, D), BF16)],
        ),
        out_shape=jax.ShapeDtypeStruct((R, W), jnp.uint32),
        compiler_params=_cparams(("arbitrary",)),
        name="moe_experts",
    )(tile_expert, n_active, xs, wg, wu, wd)


def _combine_kernel(dest_ref, x_ref, route_ref, y_ref, out_ref, ya_ref, yb_ref, sem):
    tm = x_ref.shape[0]
    bufs = (ya_ref, yb_ref)
    _row_copies(tm, lambda t, slot: pltpu.make_async_copy(
        y_ref.at[pl.ds(dest_ref[0, 2 * t + slot], 1)], bufs[slot].at[pl.ds(t, 1)], sem))
    route = route_ref[...]
    lane = _lane(route.shape)
    w_1 = jnp.sum(jnp.where(lane == 2, route, 0.0), axis=-1, keepdims=True)
    w_2 = jnp.sum(jnp.where(lane == 3, route, 0.0), axis=-1, keepdims=True)
    out_ref[...] = x_ref[...] + w_1 * _unpack_pairs(ya_ref[...]) + w_2 * _unpack_pairs(yb_ref[...])


def _combine(x, route, y, dest, tm=256):
    S, D = x.shape
    return pl.pallas_call(
        _combine_kernel,
        grid=(S // tm,),
        in_specs=[
            pl.BlockSpec((None, 1, 2 * tm), lambda i: (i, 0, 0), memory_space=pltpu.SMEM),
            pl.BlockSpec((tm, D), lambda i: (i, 0)),
            pl.BlockSpec((tm, LANES), lambda i: (i, 0)),
            pl.BlockSpec(memory_space=pl.ANY),
        ],
        out_specs=pl.BlockSpec((tm, D), lambda i: (i, 0)),
        out_shape=jax.ShapeDtypeStruct((S, D), F32),
        scratch_shapes=[pltpu.VMEM((tm, D // 2), jnp.uint32), pltpu.VMEM((tm, D // 2), jnp.uint32),
                        pltpu.SemaphoreType.DMA(())],
        compiler_params=_cparams(("arbitrary",)),
        name="moe_combine",
    )(dest.reshape(S // tm, 1, 2 * tm), x, route, y)


def _moe(x, xn, route, wg, wu, wd, layer, tile=256):
    S = x.shape[0]
    n_rows = 2 * S + N_EXPERTS * tile
    dest, tile_expert, n_active = _route_plan(route, tile, n_rows // tile)
    xs = _dispatch(xn, dest, n_rows)
    y = _experts(xs, tile_expert, n_active, wg, wu, wd, layer, tile)
    return _combine(x, route, y, dest)


def _pad_last(w, width):
    return jnp.pad(w, [(0, 0)] * (w.ndim - 1) + [(0, width - w.shape[-1])])


def _pad_heads(w):
    lead = w.shape[:-1]
    w = w.reshape(*lead, ML_HEADS, ML_HD)
    return _pad_last(w, ML_HP).reshape(*lead, ML_WP)


def _t5_buckets(rel):
    half = N_BUCKETS // 2
    max_exact = half // 2
    n = np.abs(rel)
    large = max_exact + (np.log(np.maximum(n, 1) / max_exact) / np.log(BUCKET_MAX_DIST / max_exact)
                         * (half - max_exact)).astype(np.int32)
    large = np.minimum(large, half - 1)
    return (rel > 0).astype(np.int32) * half + np.where(n < max_exact, n, large).astype(np.int32)


def _band_bias(rel_bias, d, tq=128):
    win = tq + 2 * DIL_R
    offs = (np.arange(2 * DIL_R + 1) - DIL_R) * d
    t = rel_bias.astype(F32)[jnp.asarray(_t5_buckets(offs))] * LOG2E
    v = jnp.concatenate([t, jnp.full((win + 1 - t.shape[0], DIL_HEADS), NEG, F32)], axis=0)
    b = jnp.tile(v, (tq, 1))[:tq * win].reshape(tq, win, DIL_HEADS)
    return jnp.transpose(b, (2, 0, 1))


def _prep_weights(p):
    w_in = p["w_in"]
    off = np.cumsum((0, MLA_Q_RANK, MLA_KV_RANK, MLA_ROPE, DIL_W, DIL_W, DIL_W, ML_W, ML_W, ML_W, 4 * ML_HEADS))
    col = lambda a, b: w_in[:, :, off[a]:off[b]]
    w_main = jnp.concatenate([
        col(3, 6),
        _pad_last(col(0, 3), DIL_W),
        _pad_last(col(6, 7), ML_WP),
        _pad_heads(col(7, 8)), _pad_heads(col(8, 9)),
    ], axis=-1).astype(BF16)
    w_gate = _pad_last(col(9, 10), LANES).astype(BF16)
    L = w_in.shape[0]
    row = lambda g: g[:, None, :]

    head_block = lambda w: _pad_last(w, LANES).reshape(L, w.shape[1], MLA_HEADS * LANES).astype(BF16)
    wq = head_block(p["mla_w_uq"].reshape(L, MLA_Q_RANK, MLA_HEADS, MLA_QK))
    wkv = p["mla_w_ukv"].reshape(L, MLA_KV_RANK, MLA_HEADS, MLA_NOPE + MLA_V)
    wk = head_block(wkv[..., :MLA_NOPE])
    wv = head_block(wkv[..., MLA_NOPE:])

    eye = jnp.eye(ML_HEADS, dtype=F32)

    def block_diag(w):
        bd = w[:, :, :, None, :] * eye[None, :, None, :, None]
        bd = _pad_last(bd, ML_HP).reshape(L, ML_W, ML_WP)
        return jnp.pad(bd, ((0, 0), (0, ML_WP - ML_W), (0, 0)))

    wqk = jnp.concatenate([block_diag(p["ml_w_q"]), block_diag(p["ml_w_k"]) * (ML_HD ** -0.5)],
                          axis=-1).astype(BF16)
    cw = jnp.pad(p["ml_conv_w"], ((0, 0), (0, HALO - ML_CONV), (0, ML_WP - ML_W)))

    w_out = p["w_out"]
    n_mla = MLA_HEADS * MLA_V
    w3 = w_out[:, n_mla + DIL_W:].reshape(L, ML_HEADS, ML_HD, D_MODEL)
    w3 = jnp.pad(w3, ((0, 0), (0, 0), (0, ML_HP - ML_HD), (0, 0))).reshape(L, ML_WP, D_MODEL)

    wr = _pad_last(jnp.concatenate([p["router_group_w"], p["router_expert_w"]], axis=-1), LANES)
    br = _pad_last(jnp.concatenate([p["router_group_b"], p["router_expert_b"]], axis=-1), LANES)
    return dict(
        norm_mix=row(p["norm_mix"]), w_main=w_main, w_gate=w_gate,
        gcq=row(p["mla_norm_cq"]), wq=wq, gckv=row(p["mla_norm_ckv"]), wk=wk, wv=wv,
        gq=row(_pad_last(p["mla_q_norm"], LANES)), gk=row(_pad_last(p["mla_k_norm"], LANES)),
        dgq=row(jnp.tile(p["dil_q_norm"], (1, 2))), dgk=row(jnp.tile(p["dil_k_norm"], (1, 2))),
        cw=cw, cb=row(_pad_last(p["ml_conv_b"], ML_WP)), wqk=wqk,
        gb=row(_pad_last(p["ml_gate_bias"].reshape(L, 4 * ML_HEADS), LANES)),
        gout=row(_pad_heads(p["ml_out_norm"])),
        w1=w_out[:, :n_mla].astype(BF16), w2=w_out[:, n_mla:n_mla + DIL_W].astype(BF16), w3=w3.astype(BF16),
        norm_ffn=row(p["norm_ffn"]), wr=jnp.stack(_split3(wr), axis=1), br=row(br),
    )


def kernel(x, positions, rel_bias, norm_mix, w_in, mla_norm_cq, mla_w_uq, mla_norm_ckv, mla_w_ukv, mla_q_norm, mla_k_norm, dil_q_norm, dil_k_norm, ml_conv_w, ml_conv_b, ml_w_q, ml_w_k, ml_gate_bias, ml_out_norm, w_out, norm_ffn, router_group_w, router_group_b, router_expert_w, router_expert_b, moe_w_gate, moe_w_up, moe_w_down):
    params = dict(norm_mix=norm_mix, w_in=w_in, mla_norm_cq=mla_norm_cq, mla_w_uq=mla_w_uq,
                  mla_norm_ckv=mla_norm_ckv, mla_w_ukv=mla_w_ukv, mla_q_norm=mla_q_norm, mla_k_norm=mla_k_norm,
                  dil_q_norm=dil_q_norm, dil_k_norm=dil_k_norm, ml_conv_w=ml_conv_w, ml_conv_b=ml_conv_b,
                  ml_w_q=ml_w_q, ml_w_k=ml_w_k, ml_gate_bias=ml_gate_bias, ml_out_norm=ml_out_norm, w_out=w_out,
                  norm_ffn=norm_ffn, router_group_w=router_group_w, router_group_b=router_group_b,
                  router_expert_w=router_expert_w, router_expert_b=router_expert_b)
    B, S, D = x.shape
    assert B == 1 and D == D_MODEL
    depth = w_in.shape[0]
    xs = x.reshape(S, D)
    pos = positions.reshape(S, 1)

    half = MLA_ROPE // 2
    inv_freq = ROPE_THETA ** (-jnp.arange(half, dtype=F32) / half)
    invf = jnp.zeros((LANES,), F32).at[MLA_NOPE:MLA_NOPE + half].set(inv_freq)
    invf = invf.at[MLA_NOPE + half:MLA_QK].set(inv_freq)[None, :]
    biases = [_band_bias(rel_bias, d) for _, d in DIL_PATTERNS]
    expand = (jnp.arange(LANES)[:, None] == (jnp.arange(DIL_W) // DIL_HD)[None, :]).astype(BF16)
    vone = jnp.tile((jnp.arange(LANES) == MLA_V).astype(F32), MLA_HEADS)[None, :]

    w = _prep_weights(params)
    for l in range(depth):
        main, gates = _inproj(xs, w["norm_mix"], w["w_main"], w["w_gate"], l)
        q, k, vt = _mla_prep(main, pos, invf, w["gcq"], w["wq"], w["gckv"], w["wk"], w["wv"], vone,
                             w["gq"], w["gk"], l)
        ymla = _mla_attn(q, k, vt)
        dqkv = _dil_norm(main, w["dgq"], w["dgk"], l)
        dil = [_dil_band(*dqkv[3 * i:3 * i + 3], b) for i, b in enumerate(biases)]
        mq, mk, gcol, grow = _ml_prep(main, gates, w["cw"], w["cb"], w["wqk"], w["gb"], l)
        hf, hb = _ml_scan(mq, mk, main, gcol, grow)
        xs = _outproj(xs, ymla, [o for o, _ in dil], [s for _, s in dil], hf, hb, main, w["gout"], expand,
                      w["w1"], w["w2"], w["w3"], l)
        xn, route = _router(xs, w["norm_ffn"], w["wr"], w["br"], l)
        xs = _moe(xs, xn, route, moe_w_gate, moe_w_up, moe_w_down, l)
    return xs.reshape(B, S, D)
```
